```python
import jax, jax.numpy as jnp
from jax import lax
import numpy as np

D_MODEL = 2048
BATCH = 2
SEQ = 16384
DEPTH = 1

D_MIX = D_MODEL
HEAD_DIM = 128
ATT_HEADS = (D_MIX // 2) // HEAD_DIM
D_ATT = ATT_HEADS * HEAD_DIM
D_RNN = D_MIX - D_ATT
RNN_BLOCKS = 8
RNN_BLOCK = D_RNN // RNN_BLOCKS
CONV_WIDTH = 4
LRU_C = 8.0
Q_BLOCK = 128
COL_Q = 0
COL_K = COL_Q + D_ATT
COL_V = COL_K + D_ATT
COL_F = COL_V + D_ATT
COL_RX = COL_F + ATT_HEADS
COL_RG = COL_RX + D_RNN
D_IN_PROJ = COL_RG + D_RNN
N_GROUPS = 4
EXPERTS_PER_GROUP = 8
N_EXPERTS = N_GROUPS * EXPERTS_PER_GROUP
TOP_K_INNER = 2
D_EXPERT = D_MODEL // 8
D_PLE = 256
FORGET_BIAS_INIT = 2.0
EPS = 1e-6

kernel_name = "hymba_fox_rglru_hmoe_layer"


def rms_norm(x, gain):
    x32 = x.astype(jnp.float32)
    y = x32 * lax.rsqrt(jnp.mean(x32 * x32, axis=-1, keepdims=True) + EPS)
    return (y * gain.astype(jnp.float32)).astype(x.dtype)


def forgetting_attention(q, k, v, log_f):
    b, s, h, dh = q.shape
    nb = s // Q_BLOCK
    scale = dh ** -0.5
    c = jnp.cumsum(log_f, axis=1).transpose(0, 2, 1)
    kh = k.transpose(0, 2, 1, 3)
    vh = v.transpose(0, 2, 1, 3)
    q_blocks = q.transpose(0, 2, 1, 3).reshape(b, h, nb, Q_BLOCK, dh).transpose(2, 0, 1, 3, 4)
    c_blocks = c.reshape(b, h, nb, Q_BLOCK).transpose(2, 0, 1, 3)
    k_pos = jnp.arange(s, dtype=jnp.int32)
    starts = jnp.arange(nb, dtype=jnp.int32) * Q_BLOCK

    def one_block(args):
        qb, cb, start = args
        logits = jnp.einsum('bhqd,bhkd->bhqk', qb, kh,
                            preferred_element_type=jnp.float32) * scale
        logits = logits + cb[..., :, None] - c[:, :, None, :]
        q_pos = start + jnp.arange(Q_BLOCK, dtype=jnp.int32)
        causal = k_pos[None, :] <= q_pos[:, None]
        logits = jnp.where(causal, logits, -jnp.inf)
        probs = jax.nn.softmax(logits, axis=-1)
        return jnp.einsum('bhqk,bhkd->bhqd', probs.astype(vh.dtype), vh)

    out = lax.map(one_block, (q_blocks, c_blocks, starts))
    return out.transpose(1, 0, 3, 2, 4).reshape(b, s, h * dh)


def causal_depthwise_conv(x, w, bias):
    y = lax.conv_general_dilated(
        x, w[:, None, :].astype(x.dtype), window_strides=(1,),
        padding=[(CONV_WIDTH - 1, 0)],
        dimension_numbers=('NWC', 'WIO', 'NWC'),
        feature_group_count=x.shape[-1])
    return y + bias


def rg_lru(x, w_a, b_a, w_i, b_i, lam):
    b, s, d = x.shape
    xb = x.reshape(b, s, RNN_BLOCKS, RNN_BLOCK)
    r = jax.nn.sigmoid(jnp.einsum('bsni,nij->bsnj', xb, w_a).reshape(b, s, d) + b_a)
    i = jax.nn.sigmoid(jnp.einsum('bsni,nij->bsnj', xb, w_i).reshape(b, s, d) + b_i)
    log_a = -LRU_C * r.astype(jnp.float32) * jax.nn.softplus(-lam.astype(jnp.float32))
    a = jnp.exp(log_a)
    u = jnp.sqrt(-jnp.expm1(2.0 * log_a)) * (i * x).astype(jnp.float32)

    def combine(left, right):
        a1, b1 = left
        a2, b2 = right
        return a1 * a2, a2 * b1 + b2

    _, h = lax.associative_scan(combine, (a, u), axis=1)
    return h.astype(x.dtype)


def hierarchical_moe(hn, w_router_group, w_router_expert, w_gate, w_up, w_down):
    b, s, d = hn.shape
    t = hn.reshape(b * s, d)
    g_probs = jax.nn.softmax(jnp.dot(t, w_router_group, preferred_element_type=jnp.float32), axis=-1)
    g_w, g_idx = lax.top_k(g_probs, 1)
    e_logits = jnp.dot(t, w_router_expert, preferred_element_type=jnp.float32)
    e_logits = e_logits.reshape(-1, N_GROUPS, EXPERTS_PER_GROUP)
    e_in_group = jnp.take_along_axis(e_logits, g_idx[:, :, None], axis=1)[:, 0]
    top_l, top_i = lax.top_k(e_in_group, TOP_K_INNER)
    weights = jax.nn.softmax(top_l, axis=-1) * g_w
    ids = g_idx * EXPERTS_PER_GROUP + top_i
    comb = jnp.sum(jax.nn.one_hot(ids, N_EXPERTS, dtype=jnp.float32) * weights[..., None], axis=1)
    out = jnp.zeros((b * s, d), jnp.float32)
    for e in range(N_EXPERTS):
        hid = jax.nn.silu(t @ w_gate[e]) * (t @ w_up[e])
        out = out + (hid @ w_down[e]).astype(jnp.float32) * comb[:, e:e + 1]
    return out.astype(hn.dtype).reshape(b, s, d)


def setup_inputs(seed: int = 0) -> dict:
    key = jax.random.key(seed)
    ks = jax.random.split(key, 32)
    f32 = jnp.float32

    def nrm(k, shape, scale):
        return jax.random.normal(k, shape, f32) * scale

    def gain(k, shape):
        return 1.0 + 0.02 * jax.random.normal(k, shape, f32)

    u = jax.random.uniform(ks[12], (DEPTH, D_RNN), f32, 0.9, 0.999)
    s_lam = u ** (1.0 / LRU_C)
    lru_lambda = jnp.log(s_lam) - jnp.log1p(-s_lam)
    return {
        "x": nrm(ks[0], (BATCH, SEQ, D_MODEL), 1.0),
        "p": nrm(ks[1], (DEPTH, BATCH, SEQ, D_PLE), 1.0),
        "mix_norm": gain(ks[2], (DEPTH, D_MODEL)),
        "w_in": nrm(ks[3], (DEPTH, D_MODEL, D_IN_PROJ), D_MODEL ** -0.5),
        "b_forget": FORGET_BIAS_INIT + 0.1 * jax.random.normal(ks[4], (DEPTH, ATT_HEADS), f32),
        "q_norm": gain(ks[5], (DEPTH, HEAD_DIM)),
        "k_norm": gain(ks[6], (DEPTH, HEAD_DIM)),
        "conv_w": nrm(ks[7], (DEPTH, CONV_WIDTH, D_RNN), CONV_WIDTH ** -0.5),
        "conv_b": nrm(ks[8], (DEPTH, D_RNN), 0.02),
        "w_rec_gate": nrm(ks[9], (DEPTH, RNN_BLOCKS, RNN_BLOCK, RNN_BLOCK), RNN_BLOCK ** -0.5),
        "b_rec_gate": nrm(ks[10], (DEPTH, D_RNN), 0.02),
        "w_in_gate": nrm(ks[11], (DEPTH, RNN_BLOCKS, RNN_BLOCK, RNN_BLOCK), RNN_BLOCK ** -0.5),
        "b_in_gate": nrm(ks[13], (DEPTH, D_RNN), 0.02),
        "lru_lambda": lru_lambda,
        "w_out": nrm(ks[14], (DEPTH, D_MIX, D_MODEL), D_MIX ** -0.5),
        "ffn_norm": gain(ks[15], (DEPTH, D_MODEL)),
        "w_router_group": nrm(ks[16], (DEPTH, D_MODEL, N_GROUPS), D_MODEL ** -0.5),
        "w_router_expert": nrm(ks[17], (DEPTH, D_MODEL, N_EXPERTS), D_MODEL ** -0.5),
        "w_expert_gate": nrm(ks[18], (DEPTH, N_EXPERTS, D_MODEL, D_EXPERT), D_MODEL ** -0.5),
        "w_expert_up": nrm(ks[19], (DEPTH, N_EXPERTS, D_MODEL, D_EXPERT), D_MODEL ** -0.5),
        "w_expert_down": nrm(ks[20], (DEPTH, N_EXPERTS, D_EXPERT, D_MODEL), D_EXPERT ** -0.5),
        "ple_norm": gain(ks[21], (DEPTH, D_MODEL)),
        "w_ple_gate": nrm(ks[22], (DEPTH, D_MODEL, D_MODEL), D_MODEL ** -0.5),
        "w_ple_up": nrm(ks[23], (DEPTH, D_PLE, D_MODEL), D_PLE ** -0.5),
    }


def reference(x, p, mix_norm, w_in, b_forget, q_norm, k_norm, conv_w, conv_b,
              w_rec_gate, b_rec_gate, w_in_gate, b_in_gate, lru_lambda, w_out,
              ffn_norm, w_router_group, w_router_expert, w_expert_gate,
              w_expert_up, w_expert_down, ple_norm, w_ple_gate, w_ple_up):
    h = x
    b, s, _ = x.shape
    for i in range(DEPTH):
        hn = rms_norm(h, mix_norm[i])
        proj = hn @ w_in[i]
        q = rms_norm(proj[..., COL_Q:COL_K].reshape(b, s, ATT_HEADS, HEAD_DIM), q_norm[i])
        k = rms_norm(proj[..., COL_K:COL_V].reshape(b, s, ATT_HEADS, HEAD_DIM), k_norm[i])
        v = proj[..., COL_V:COL_F].reshape(b, s, ATT_HEADS, HEAD_DIM)
        log_f = jax.nn.log_sigmoid((proj[..., COL_F:COL_RX] + b_forget[i]).astype(jnp.float32))
        att_out = forgetting_attention(q, k, v, log_f)

        rx = causal_depthwise_conv(proj[..., COL_RX:COL_RG], conv_w[i], conv_b[i])
        rnn = rg_lru(rx, w_rec_gate[i], b_rec_gate[i], w_in_gate[i], b_in_gate[i], lru_lambda[i])
        rnn_out = jax.nn.gelu(proj[..., COL_RG:D_IN_PROJ], approximate=True) * rnn

        h = h + jnp.concatenate([att_out, rnn_out], axis=-1) @ w_out[i]

        hn = rms_norm(h, ffn_norm[i])
        h = h + hierarchical_moe(hn, w_router_group[i], w_router_expert[i],
                                 w_expert_gate[i], w_expert_up[i], w_expert_down[i])

        gate = jax.nn.sigmoid(rms_norm(h, ple_norm[i]) @ w_ple_gate[i])
        h = h + gate * (p[i] @ w_ple_up[i])
    return h
```

```python
import functools
import math

import jax
import jax.numpy as jnp
import numpy as np
from jax import lax
from jax.experimental import pallas as pl
from jax.experimental.pallas import tpu as pltpu

HEAD_DIM = 128
RNN_BLOCK = 128
CONV_WIDTH = 4
LRU_C = 8.0
N_GROUPS = 4
EXPERTS_PER_GROUP = 8
EPS = 1e-6
LANES = 128
SUBLANES = 8
VMEM_LIMIT = 56 * 1024 * 1024

F32 = jnp.float32
BF16 = jnp.bfloat16


def _params(sem):
    return pltpu.CompilerParams(dimension_semantics=sem, vmem_limit_bytes=VMEM_LIMIT)


def _rms(x, gain):
    return x * lax.rsqrt(jnp.mean(x * x, axis=-1, keepdims=True) + EPS) * gain


def _in_proj_kernel(x_ref, g_ref, w_ref, wf_ref, qg_ref, kg_ref,
                    q_ref, k_ref, v_ref, rx_ref, rg_ref, f_ref, xn_ref, *, n_heads):
    j = pl.program_id(1)

    @pl.when(j == 0)
    def _():
        xn = _rms(x_ref[...], g_ref[...]).astype(BF16)
        xn_ref[...] = xn
        f_ref[...] = jnp.dot(xn, wf_ref[...], preferred_element_type=F32)

    acc = jnp.dot(xn_ref[...], w_ref[...], preferred_element_type=F32)

    def head_norm(gain_ref, out_ref):
        for hd in range(n_heads):
            sl = slice(hd * HEAD_DIM, (hd + 1) * HEAD_DIM)
            out_ref[:, sl] = _rms(acc[:, sl], gain_ref[...]).astype(out_ref.dtype)

    @pl.when(j == 0)
    def _():
        head_norm(qg_ref, q_ref)

    @pl.when(j == 1)
    def _():
        head_norm(kg_ref, k_ref)

    @pl.when(j == 2)
    def _():
        v_ref[...] = acc.astype(v_ref.dtype)

    @pl.when(j == 3)
    def _():
        rx_ref[...] = acc.astype(rx_ref.dtype)

    @pl.when(j == 4)
    def _():
        rg_ref[...] = acc.astype(rg_ref.dtype)


def _in_proj(x2, gain, w_main, w_f, q_gain, k_gain, *, tm):
    t, d = x2.shape
    d_att = w_main.shape[1] // 5
    n_heads = d_att // HEAD_DIM
    row = lambda i, j: (i, 0)
    fixed = lambda i, j: (0, 0)
    big = jax.ShapeDtypeStruct((t, d_att), BF16)
    return pl.pallas_call(
        functools.partial(_in_proj_kernel, n_heads=n_heads),
        grid=(t // tm, 5),
        in_specs=[
            pl.BlockSpec((tm, d), row),
            pl.BlockSpec((1, d), fixed),
            pl.BlockSpec((d, d_att), lambda i, j: (0, j)),
            pl.BlockSpec((d, LANES), fixed),
            pl.BlockSpec((1, HEAD_DIM), fixed),
            pl.BlockSpec((1, HEAD_DIM), fixed),
        ],
        out_specs=[pl.BlockSpec((tm, d_att), row)] * 5 + [pl.BlockSpec((tm, LANES), row)],
        out_shape=[big] * 5 + [jax.ShapeDtypeStruct((t, LANES), F32)],
        scratch_shapes=[pltpu.VMEM((tm, d), BF16)],
        compiler_params=_params(("arbitrary", "arbitrary")),
        name="in_proj",
    )(x2, gain, w_main, w_f, q_gain, k_gain)


def _forget_cum_kernel(f_ref, b_ref, c_ref, ct_ref, carry_ref):
    @pl.when(pl.program_id(1) == 0)
    def _():
        carry_ref[...] = jnp.zeros_like(carry_ref)

    log_f = jax.nn.log_sigmoid(f_ref[0] + b_ref[...])
    tc = log_f.shape[0]
    tri = (lax.broadcasted_iota(jnp.int32, (tc, tc), 0)
           >= lax.broadcasted_iota(jnp.int32, (tc, tc), 1)).astype(F32)
    c = jnp.dot(tri, log_f, preferred_element_type=F32,
                precision=lax.Precision.HIGHEST) + carry_ref[...]
    c_ref[0] = c
    ct_ref[0] = c.T[:SUBLANES, :]
    carry_ref[...] = c[tc - 1:tc, :]


def _forget_cum(f_logits, b_pad, *, tc):
    b, s, _ = f_logits.shape
    return pl.pallas_call(
        _forget_cum_kernel,
        grid=(b, s // tc),
        in_specs=[pl.BlockSpec((1, tc, LANES), lambda bi, i: (bi, i, 0)),
                  pl.BlockSpec((1, LANES), lambda bi, i: (0, 0))],
        out_specs=[pl.BlockSpec((1, tc, LANES), lambda bi, i: (bi, i, 0)),
                   pl.BlockSpec((1, SUBLANES, tc), lambda bi, i: (bi, 0, i))],
        out_shape=[jax.ShapeDtypeStruct((b, s, LANES), F32),
                   jax.ShapeDtypeStruct((b, SUBLANES, s), F32)],
        scratch_shapes=[pltpu.VMEM((1, LANES), F32)],
        compiler_params=_params(("arbitrary", "arbitrary")),
        name="forget_cum",
    )(f_logits, b_pad)


def _attn_kernel(qi_ref, kj_ref, q_ref, k_ref, v_ref, c_ref, ct_ref, o_ref,
                 m_ref, l_ref, acc_ref):
    h = pl.program_id(1)
    p = pl.program_id(2)
    i = qi_ref[p]
    j = kj_ref[p]

    @pl.when(j == 0)
    def _():
        m_ref[...] = jnp.full_like(m_ref, -jnp.inf)
        l_ref[...] = jnp.zeros_like(l_ref)
        acc_ref[...] = jnp.zeros_like(acc_ref)

    def step(masked):
        q = q_ref[0]
        k = k_ref[0]
        tq, tk = q.shape[0], k.shape[0]
        s = lax.dot_general(q, k, (((1,), (1,)), ((), ())), preferred_element_type=F32)
        c_blk = c_ref[0]
        lane = lax.broadcasted_iota(jnp.int32, c_blk.shape, 1)
        c_col = jnp.sum(jnp.where(lane == h, c_blk, 0.0), axis=-1, keepdims=True)
        c_row = ct_ref[0, pl.ds(h, 1), :]
        s = s + (c_col - c_row)
        if masked:
            rows = lax.broadcasted_iota(jnp.int32, (tq, tk), 0)
            cols = lax.broadcasted_iota(jnp.int32, (tq, tk), 1)
            s = jnp.where(rows >= cols, s, -jnp.inf)
        m_prev = m_ref[...]
        m_new = jnp.maximum(m_prev, jnp.max(s, axis=-1, keepdims=True))
        alpha = jnp.exp(m_prev - m_new)
        pr = jnp.exp(s - m_new)
        l_ref[...] = alpha * l_ref[...] + jnp.sum(pr, axis=-1, keepdims=True)
        acc_ref[...] = alpha * acc_ref[...] + jnp.dot(
            pr.astype(BF16), v_ref[0], preferred_element_type=F32)
        m_ref[...] = m_new

    @pl.when(j < i)
    def _():
        step(False)

    @pl.when(j == i)
    def _():
        step(True)
        o_ref[0] = (acc_ref[...] / l_ref[...]).astype(o_ref.dtype)


def _fox_attn(q, k, v, c, ct, *, tq):
    b, s, d_att = q.shape
    n_heads = d_att // HEAD_DIM
    nq = s // tq
    pairs = [(i, j) for i in range(nq) for j in range(i + 1)]
    qi = jnp.asarray(np.array([pq[0] for pq in pairs], np.int32))
    kj = jnp.asarray(np.array([pq[1] for pq in pairs], np.int32))
    q_map = lambda bi, h, p, qi, kj: (bi, qi[p], h)
    k_map = lambda bi, h, p, qi, kj: (bi, kj[p], h)
    grid_spec = pltpu.PrefetchScalarGridSpec(
        num_scalar_prefetch=2,
        grid=(b, n_heads, len(pairs)),
        in_specs=[
            pl.BlockSpec((1, tq, HEAD_DIM), q_map),
            pl.BlockSpec((1, tq, HEAD_DIM), k_map),
            pl.BlockSpec((1, tq, HEAD_DIM), k_map),
            pl.BlockSpec((1, tq, LANES), lambda bi, h, p, qi, kj: (bi, qi[p], 0)),
            pl.BlockSpec((1, SUBLANES, tq), lambda bi, h, p, qi, kj: (bi, 0, kj[p])),
        ],
        out_specs=pl.BlockSpec((1, tq, HEAD_DIM), q_map),
        scratch_shapes=[pltpu.VMEM((tq, 1), F32), pltpu.VMEM((tq, 1), F32),
                        pltpu.VMEM((tq, HEAD_DIM), F32)],
    )
    return pl.pallas_call(
        _attn_kernel,
        grid_spec=grid_spec,
        out_shape=jax.ShapeDtypeStruct((b, s, d_att), BF16),
        compiler_params=_params(("arbitrary", "arbitrary", "arbitrary")),
        name="fox_attn",
    )(qi, kj, q, k, v, c, ct)


def _rglru_kernel(rx_ref, rg_ref, cw_ref, cb_ref, wa_ref, ba_ref, wi_ref, bi_ref, lam_ref,
                  o_ref, xbuf_ref, a_ref, u_ref, h_ref, *, n_blocks):
    tc = rx_ref.shape[1]

    @pl.when(pl.program_id(1) == 0)
    def _():
        xbuf_ref[0:SUBLANES, :] = jnp.zeros((SUBLANES, xbuf_ref.shape[1]), F32)
        h_ref[...] = jnp.zeros_like(h_ref)

    xbuf_ref[SUBLANES:SUBLANES + tc, :] = rx_ref[0].astype(F32)
    y = cb_ref[...]
    for kk in range(CONV_WIDTH):
        off = SUBLANES - (CONV_WIDTH - 1) + kk
        y = y + cw_ref[kk:kk + 1, :] * xbuf_ref[off:off + tc, :]
    xbuf_ref[0:SUBLANES, :] = xbuf_ref[tc:tc + SUBLANES, :]

    yb = y.astype(BF16)
    neg_sp = -LRU_C * jax.nn.softplus(-lam_ref[...])
    for n in range(n_blocks):
        sl = slice(n * RNN_BLOCK, (n + 1) * RNN_BLOCK)
        r = jax.nn.sigmoid(jnp.dot(yb[:, sl], wa_ref[n], preferred_element_type=F32) + ba_ref[:, sl])
        g = jax.nn.sigmoid(jnp.dot(yb[:, sl], wi_ref[n], preferred_element_type=F32) + bi_ref[:, sl])
        log_a = r * neg_sp[:, sl]
        a = jnp.exp(log_a)
        a_ref[:, sl] = a
        u_ref[:, sl] = jnp.sqrt(1.0 - a * a) * (g * y[:, sl])

    def scan_body(t, hcur):
        hnew = a_ref[pl.ds(t, 1), :] * hcur + u_ref[pl.ds(t, 1), :]
        u_ref[pl.ds(t, 1), :] = hnew
        return hnew

    h_ref[...] = lax.fori_loop(0, tc, scan_body, h_ref[...], unroll=8)

    o_ref[0] = (jax.nn.gelu(rg_ref[0].astype(F32), approximate=True) * u_ref[...]).astype(o_ref.dtype)


def _rglru(rx, rg, conv_w, conv_b, w_a, b_a, w_i, b_i, lam, *, tc):
    b, s, c = rx.shape
    n_blocks = c // RNN_BLOCK
    seq = lambda bi, i: (bi, i, 0)
    fixed2 = lambda bi, i: (0, 0)
    fixed3 = lambda bi, i: (0, 0, 0)
    return pl.pallas_call(
        functools.partial(_rglru_kernel, n_blocks=n_blocks),
        grid=(b, s // tc),
        in_specs=[
            pl.BlockSpec((1, tc, c), seq),
            pl.BlockSpec((1, tc, c), seq),
            pl.BlockSpec((CONV_WIDTH, c), fixed2),
            pl.BlockSpec((1, c), fixed2),
            pl.BlockSpec((n_blocks, RNN_BLOCK, RNN_BLOCK), fixed3),
            pl.BlockSpec((1, c), fixed2),
            pl.BlockSpec((n_blocks, RNN_BLOCK, RNN_BLOCK), fixed3),
            pl.BlockSpec((1, c), fixed2),
            pl.BlockSpec((1, c), fixed2),
        ],
        out_specs=pl.BlockSpec((1, tc, c), seq),
        out_shape=jax.ShapeDtypeStruct((b, s, c), BF16),
        scratch_shapes=[pltpu.VMEM((tc + SUBLANES, c), F32), pltpu.VMEM((tc, c), F32),
                        pltpu.VMEM((tc, c), F32), pltpu.VMEM((1, c), F32)],
        compiler_params=_params(("arbitrary", "arbitrary")),
        name="rglru",
    )(rx, rg, conv_w, conv_b, w_a, b_a, w_i, b_i, lam)


def _out_proj_kernel(x_ref, att_ref, rnn_ref, wo_ref, g_ref, wr_ref,
                     h_ref, hn_ref, comb_ref, *, d_att, n_experts):
    h = (x_ref[...]
         + jnp.dot(att_ref[...], wo_ref[0:d_att, :], preferred_element_type=F32)
         + jnp.dot(rnn_ref[...], wo_ref[d_att:, :], preferred_element_type=F32))
    h_ref[...] = h
    hn = _rms(h, g_ref[...])
    hn_ref[...] = hn.astype(hn_ref.dtype)

    logits = jnp.dot(hn, wr_ref[...], preferred_element_type=F32, precision=lax.Precision.HIGHEST)
    lane = lax.broadcasted_iota(jnp.int32, logits.shape, 1)
    neg = -jnp.inf

    def first_argmax(vals, vmax):
        return jnp.min(jnp.where(vals == vmax, lane, LANES), axis=-1, keepdims=True)

    g_mask = (lane >= n_experts) & (lane < n_experts + N_GROUPS)
    g_log = jnp.where(g_mask, logits, neg)
    g_max = jnp.max(g_log, axis=-1, keepdims=True)
    g_w = 1.0 / jnp.sum(jnp.exp(g_log - g_max), axis=-1, keepdims=True)
    g_idx = first_argmax(g_log, g_max) - n_experts

    e_lo = g_idx * EXPERTS_PER_GROUP
    e_log = jnp.where((lane >= e_lo) & (lane < e_lo + EXPERTS_PER_GROUP), logits, neg)
    l1 = jnp.max(e_log, axis=-1, keepdims=True)
    i1 = first_argmax(e_log, l1)
    e_log2 = jnp.where(lane == i1, neg, e_log)
    l2 = jnp.max(e_log2, axis=-1, keepdims=True)
    i2 = first_argmax(e_log2, l2)
    e2 = jnp.exp(l2 - l1)
    w1 = g_w / (1.0 + e2)
    w2 = g_w * e2 / (1.0 + e2)
    comb_ref[...] = jnp.where(lane == i1, w1, 0.0) + jnp.where(lane == i2, w2, 0.0)


def _out_proj(x2, att, rnn, w_out, gain, w_router, *, tm, n_experts):
    t, d = x2.shape
    d_att = att.shape[1]
    d_rnn = rnn.shape[1]
    row = lambda i: (i, 0)
    fixed = lambda i: (0, 0)
    return pl.pallas_call(
        functools.partial(_out_proj_kernel, d_att=d_att, n_experts=n_experts),
        grid=(t // tm,),
        in_specs=[
            pl.BlockSpec((tm, d), row),
            pl.BlockSpec((tm, d_att), row),
            pl.BlockSpec((tm, d_rnn), row),
            pl.BlockSpec((d_att + d_rnn, d), fixed),
            pl.BlockSpec((1, d), fixed),
            pl.BlockSpec((d, LANES), fixed),
        ],
        out_specs=[pl.BlockSpec((tm, d), row), pl.BlockSpec((tm, d), row),
                   pl.BlockSpec((tm, LANES), row)],
        out_shape=[jax.ShapeDtypeStruct((t, d), F32), jax.ShapeDtypeStruct((t, d), BF16),
                   jax.ShapeDtypeStruct((t, LANES), F32)],
        compiler_params=_params(("arbitrary",)),
        name="out_proj",
    )(x2, att, rnn, w_out, gain, w_router)


def _moe_kernel(hn_ref, comb_ref, h_ref, wg_ref, wu_ref, wd_ref, o_ref, acc_ref):
    e = pl.program_id(1)

    @pl.when(e == 0)
    def _():
        acc_ref[...] = jnp.zeros_like(acc_ref)

    t = hn_ref[...]
    hid = (jax.nn.silu(jnp.dot(t, wg_ref[0], preferred_element_type=F32))
           * jnp.dot(t, wu_ref[0], preferred_element_type=F32))
    comb = comb_ref[...]
    lane = lax.broadcasted_iota(jnp.int32, comb.shape, 1)
    w_col = jnp.sum(jnp.where(lane == e, comb, 0.0), axis=-1, keepdims=True)
    acc_ref[...] += jnp.dot((hid * w_col).astype(BF16), wd_ref[0], preferred_element_type=F32)

    @pl.when(e == pl.num_programs(1) - 1)
    def _():
        o_ref[...] = h_ref[...] + acc_ref[...]


def _moe(hn, comb, h, w_gate, w_up, w_down, *, tm):
    t, d = hn.shape
    n_experts, _, d_e = w_gate.shape
    row = lambda i, e: (i, 0)
    return pl.pallas_call(
        _moe_kernel,
        grid=(t // tm, n_experts),
        in_specs=[
            pl.BlockSpec((tm, d), row),
            pl.BlockSpec((tm, LANES), row),
            pl.BlockSpec((tm, d), row),
            pl.BlockSpec((1, d, d_e), lambda i, e: (e, 0, 0)),
            pl.BlockSpec((1, d, d_e), lambda i, e: (e, 0, 0)),
            pl.BlockSpec((1, d_e, d), lambda i, e: (e, 0, 0)),
        ],
        out_specs=pl.BlockSpec((tm, d), row),
        out_shape=jax.ShapeDtypeStruct((t, d), F32),
        scratch_shapes=[pltpu.VMEM((tm, d), F32)],
        compiler_params=_params(("arbitrary", "arbitrary")),
        name="moe",
    )(hn, comb, h, w_gate, w_up, w_down)


def _ple_kernel(h_ref, p_ref, g_ref, wg_ref, wu_ref, o_ref):
    h = h_ref[...]
    hn = _rms(h, g_ref[...]).astype(BF16)
    gate = jax.nn.sigmoid(jnp.dot(hn, wg_ref[...], preferred_element_type=F32))
    up = jnp.dot(p_ref[...].astype(BF16), wu_ref[...], preferred_element_type=F32)
    o_ref[...] = h + gate * up


def _ple(h, p2, gain, w_gate, w_up, *, tm):
    t, d = h.shape
    d_ple = p2.shape[1]
    row = lambda i: (i, 0)
    fixed = lambda i: (0, 0)
    return pl.pallas_call(
        _ple_kernel,
        grid=(t // tm,),
        in_specs=[
            pl.BlockSpec((tm, d), row),
            pl.BlockSpec((tm, d_ple), row),
            pl.BlockSpec((1, d), fixed),
            pl.BlockSpec((d, d), fixed),
            pl.BlockSpec((d_ple, d), fixed),
        ],
        out_specs=pl.BlockSpec((tm, d), row),
        out_shape=jax.ShapeDtypeStruct((t, d), F32),
        compiler_params=_params(("arbitrary",)),
        name="ple",
    )(h, p2, gain, w_gate, w_up)


def _tile(n, want):
    if n <= want:
        return n
    for cand in range(want, 0, -SUBLANES):
        if n % cand == 0:
            return cand
    return n


def kernel(x, p, mix_norm, w_in, b_forget, q_norm, k_norm, conv_w, conv_b, w_rec_gate, b_rec_gate,
           w_in_gate, b_in_gate, lru_lambda, w_out, ffn_norm, w_router_group, w_router_expert,
           w_expert_gate, w_expert_up, w_expert_down, ple_norm, w_ple_gate, w_ple_up):
    b, s, d = x.shape
    depth = w_in.shape[0]
    n_heads = b_forget.shape[1]
    d_att = n_heads * HEAD_DIM
    d_rnn = conv_w.shape[2]
    n_experts = w_router_expert.shape[2]
    t = b * s
    col_f = 3 * d_att
    col_rx = col_f + n_heads

    tm = _tile(t, 512)
    tq = _tile(s, 1024)
    tc_cum = _tile(s, 256)
    tc_lru = _tile(s, 256)
    tm_moe = _tile(t, 1024)

    h = x.reshape(t, d)
    for li in range(depth):
        w = w_in[li]
        w_main = jnp.concatenate([w[:, :col_f], w[:, col_rx:]], axis=1).astype(BF16)
        w_f = jnp.pad(w[:, col_f:col_rx], ((0, 0), (0, LANES - n_heads))).astype(BF16)
        q_gain = (q_norm[li] * (HEAD_DIM ** -0.5)).reshape(1, HEAD_DIM)
        k_gain = k_norm[li].reshape(1, HEAD_DIM)
        q, k, v, rx, rg, f_logits = _in_proj(
            h, mix_norm[li].reshape(1, d), w_main, w_f, q_gain, k_gain, tm=tm)

        b_pad = jnp.pad(b_forget[li], (0, LANES - n_heads)).reshape(1, LANES)
        c, ct = _forget_cum(f_logits.reshape(b, s, LANES), b_pad, tc=tc_cum)
        att = _fox_attn(q.reshape(b, s, d_att), k.reshape(b, s, d_att), v.reshape(b, s, d_att),
                        c, ct, tq=tq)

        rnn = _rglru(rx.reshape(b, s, d_rnn), rg.reshape(b, s, d_rnn), conv_w[li],
                     conv_b[li].reshape(1, d_rnn), w_rec_gate[li].astype(BF16),
                     b_rec_gate[li].reshape(1, d_rnn), w_in_gate[li].astype(BF16),
                     b_in_gate[li].reshape(1, d_rnn), lru_lambda[li].reshape(1, d_rnn), tc=tc_lru)

        w_router = jnp.pad(jnp.concatenate([w_router_expert[li], w_router_group[li]], axis=1),
                           ((0, 0), (0, LANES - n_experts - N_GROUPS)))
        h1, hn, comb = _out_proj(h, att.reshape(t, d_att), rnn.reshape(t, d_rnn),
                                 w_out[li].astype(BF16), ffn_norm[li].reshape(1, d), w_router,
                                 tm=tm, n_experts=n_experts)

        h2 = _moe(hn, comb, h1, w_expert_gate[li].astype(BF16), w_expert_up[li].astype(BF16),
                  w_expert_down[li].astype(BF16), tm=tm_moe)

        h = _ple(h2, p[li].reshape(t, -1), ple_norm[li].reshape(1, d),
                 w_ple_gate[li].astype(BF16), w_ple_up[li].astype(BF16), tm=tm)
    return h.reshape(b, s, d)
```

```python
import functools

import jax
import jax.numpy as jnp
from jax import lax
from jax.experimental import pallas as pl
from jax.experimental.pallas import tpu as pltpu

HEAD_DIM = 128
RNN_BLOCK = 128
CONV_WIDTH = 4
LRU_C = 8.0
N_GROUPS = 4
EXPERTS_PER_GROUP = 8
TOP_K = 2
EPS = 1e-6
LANES = 128
SUBLANES = 8
VMEM_LIMIT = 56 * 1024 * 1024
ZERO_PROB_LOG = 104.0
R_E1, R_E2, R_W1, R_W2, R_RANK1, R_RANK2 = range(6)
DMA_UNROLL = 8

F32 = jnp.float32
BF16 = jnp.bfloat16


def _params(sem):
    return pltpu.CompilerParams(dimension_semantics=sem, vmem_limit_bytes=VMEM_LIMIT)


def _rms(x, gain):
    return x * lax.rsqrt(jnp.mean(x * x, axis=-1, keepdims=True) + EPS) * gain


def _lane_pick(rec, lane_idx):
    lane = lax.broadcasted_iota(jnp.int32, rec.shape, 1)
    return jnp.sum(jnp.where(lane == lane_idx, rec, 0.0), axis=-1, keepdims=True)


def _in_proj_kernel(x_ref, g_ref, w_ref, wf_ref, qg_ref, kg_ref,
                    q_ref, k_ref, v_ref, rx_ref, rg_ref, f_ref, xn_ref, *, n_heads):
    j = pl.program_id(1)

    @pl.when(j == 0)
    def _():
        xn = _rms(x_ref[...], g_ref[...]).astype(BF16)
        xn_ref[...] = xn
        f_ref[...] = jnp.dot(xn, wf_ref[...], preferred_element_type=F32)

    acc = jnp.dot(xn_ref[...], w_ref[...], preferred_element_type=F32)

    def head_norm(gain_ref, out_ref):
        for hd in range(n_heads):
            sl = slice(hd * HEAD_DIM, (hd + 1) * HEAD_DIM)
            out_ref[:, sl] = _rms(acc[:, sl], gain_ref[...]).astype(out_ref.dtype)

    @pl.when(j == 0)
    def _():
        head_norm(qg_ref, q_ref)

    @pl.when(j == 1)
    def _():
        head_norm(kg_ref, k_ref)

    @pl.when(j == 2)
    def _():
        v_ref[...] = acc.astype(v_ref.dtype)

    @pl.when(j == 3)
    def _():
        rx_ref[...] = acc.astype(rx_ref.dtype)

    @pl.when(j == 4)
    def _():
        rg_ref[...] = acc.astype(rg_ref.dtype)


def _in_proj(x2, gain, w_main, w_f, q_gain, k_gain, *, tm):
    t, d = x2.shape
    d_att = w_main.shape[1] // 5
    n_heads = d_att // HEAD_DIM
    row = lambda i, j: (i, 0)
    fixed = lambda i, j: (0, 0)
    big = jax.ShapeDtypeStruct((t, d_att), BF16)
    return pl.pallas_call(
        functools.partial(_in_proj_kernel, n_heads=n_heads),
        grid=(t // tm, 5),
        in_specs=[
            pl.BlockSpec((tm, d), row),
            pl.BlockSpec((1, d), fixed),
            pl.BlockSpec((d, d_att), lambda i, j: (0, j)),
            pl.BlockSpec((d, LANES), fixed),
            pl.BlockSpec((1, HEAD_DIM), fixed),
            pl.BlockSpec((1, HEAD_DIM), fixed),
        ],
        out_specs=[pl.BlockSpec((tm, d_att), row)] * 5 + [pl.BlockSpec((tm, LANES), row)],
        out_shape=[big] * 5 + [jax.ShapeDtypeStruct((t, LANES), F32)],
        scratch_shapes=[pltpu.VMEM((tm, d), BF16)],
        compiler_params=_params(("arbitrary", "arbitrary")),
        name="in_proj",
    )(x2, gain, w_main, w_f, q_gain, k_gain)


def _forget_cum_kernel(f_ref, b_ref, ct_ref, carry_ref):
    @pl.when(pl.program_id(1) == 0)
    def _():
        carry_ref[...] = jnp.zeros_like(carry_ref)

    log_f = jax.nn.log_sigmoid(f_ref[0] + b_ref[...])
    tc = log_f.shape[0]
    tri = (lax.broadcasted_iota(jnp.int32, (tc, tc), 0)
           >= lax.broadcasted_iota(jnp.int32, (tc, tc), 1)).astype(F32)
    c = jnp.dot(tri, log_f, preferred_element_type=F32,
                precision=lax.Precision.HIGHEST) + carry_ref[...]
    ct_ref[0, 0] = c.T[:SUBLANES, :]
    carry_ref[...] = c[tc - 1:tc, :]


def _forget_cum(f_logits, b_pad, *, tc):
    b, s, _ = f_logits.shape
    return pl.pallas_call(
        _forget_cum_kernel,
        grid=(b, s // tc),
        in_specs=[pl.BlockSpec((1, tc, LANES), lambda bi, i: (bi, i, 0)),
                  pl.BlockSpec((1, LANES), lambda bi, i: (0, 0))],
        out_specs=pl.BlockSpec((1, 1, SUBLANES, tc), lambda bi, i: (bi, i, 0, 0)),
        out_shape=jax.ShapeDtypeStruct((b, s // tc, SUBLANES, tc), F32),
        scratch_shapes=[pltpu.VMEM((1, LANES), F32)],
        compiler_params=_params(("arbitrary", "arbitrary")),
        name="forget_cum",
    )(f_logits, b_pad)


def _attn_kernel(cfirst_ref, clast_ref, qkb_ref, q_ref, k_ref, v_ref, ct_ref, o_ref,
                 vaug_ref, m_ref, acc_ref, sa_ref, sb_ref):
    bi, h, i = pl.program_id(0), pl.program_id(1), pl.program_id(2)
    tq = q_ref.shape[1]
    base = (bi * pl.num_programs(1) + h) * pl.num_programs(2)

    @pl.when(i == 0)
    def _():
        vaug_ref[:, :HEAD_DIM] = v_ref[0]
        vaug_ref[:, HEAD_DIM:] = jnp.ones((vaug_ref.shape[0], HEAD_DIM), vaug_ref.dtype)

    c_ref0 = cfirst_ref[base + i]
    keep_below = c_ref0 + qkb_ref[0] + ZERO_PROB_LOG
    j_lo = lax.while_loop(
        lambda j: (j > 0) & (clast_ref[base + jnp.maximum(j - 1, 0)] < keep_below),
        lambda j: j - 1, i)

    m_ref[...] = jnp.full_like(m_ref, -jnp.inf)
    acc_ref[...] = jnp.zeros_like(acc_ref)

    def scores(j, s_ref):
        off = pl.multiple_of(j * tq, tq)
        s = lax.dot_general(q_ref[0], k_ref[0, pl.ds(off, tq), :], (((1,), (1,)), ((), ())),
                            preferred_element_type=F32)
        s_ref[...] = s + (c_ref0 - ct_ref[0, j, pl.ds(h, 1), :])

    def accumulate(j, s_ref, masked):
        off = pl.multiple_of(j * tq, tq)
        s = s_ref[...]
        if masked:
            rows = lax.broadcasted_iota(jnp.int32, s.shape, 0)
            cols = lax.broadcasted_iota(jnp.int32, s.shape, 1)
            s = jnp.where(rows >= cols, s, -jnp.inf)
        m_prev = m_ref[...]
        m_new = jnp.maximum(m_prev, jnp.max(s, axis=-1, keepdims=True))
        alpha = jnp.exp(m_prev - m_new)
        pr = jnp.exp(s - m_new).astype(BF16)
        acc_ref[...] = alpha * acc_ref[...] + jnp.dot(
            pr, vaug_ref[pl.ds(off, tq), :], preferred_element_type=F32)
        m_ref[...] = m_new

    n_off = i - j_lo
    odd = n_off % 2
    scores(j_lo, sa_ref)

    @pl.when(odd == 1)
    def _():
        accumulate(j_lo, sa_ref, False)
        scores(j_lo + 1, sa_ref)

    j_even = j_lo + odd

    def pair(kk, carry):
        j = j_even + 2 * kk
        scores(j + 1, sb_ref)
        accumulate(j, sa_ref, False)
        scores(j + 2, sa_ref)
        accumulate(j + 1, sb_ref, False)
        return carry

    lax.fori_loop(0, n_off // 2, pair, 0)
    accumulate(i, sa_ref, True)
    o_ref[0] = (acc_ref[:, :HEAD_DIM] / acc_ref[:, HEAD_DIM:]).astype(o_ref.dtype)


def _fox_attn(q, k, v, ct, c_first, c_last, qk_bound, *, tq):
    b, s, d_att = q.shape
    n_heads = d_att // HEAD_DIM
    nq = s // tq
    q_map = lambda bi, h, i, *_: (bi, i, h)
    kv_map = lambda bi, h, i, *_: (bi, 0, h)
    grid_spec = pltpu.PrefetchScalarGridSpec(
        num_scalar_prefetch=3,
        grid=(b, n_heads, nq),
        in_specs=[
            pl.BlockSpec((1, tq, HEAD_DIM), q_map),
            pl.BlockSpec((1, s, HEAD_DIM), kv_map),
            pl.BlockSpec((1, s, HEAD_DIM), kv_map),
            pl.BlockSpec((1, nq, SUBLANES, tq), lambda bi, h, i, *_: (bi, 0, 0, 0)),
        ],
        out_specs=pl.BlockSpec((1, tq, HEAD_DIM), q_map),
        scratch_shapes=[pltpu.VMEM((s, 2 * HEAD_DIM), BF16), pltpu.VMEM((tq, 1), F32),
                        pltpu.VMEM((tq, 2 * HEAD_DIM), F32), pltpu.VMEM((tq, tq), F32),
                        pltpu.VMEM((tq, tq), F32)],
    )
    return pl.pallas_call(
        _attn_kernel,
        grid_spec=grid_spec,
        out_shape=jax.ShapeDtypeStruct((b, s, d_att), BF16),
        compiler_params=_params(("arbitrary", "arbitrary", "arbitrary")),
        name="fox_attn",
    )(c_first, c_last, qk_bound, q, k, v, ct)


def _rglru_kernel(rx_ref, rg_ref, cw_ref, cb_ref, wa_ref, ba_ref, wi_ref, bi_ref, lam_ref,
                  o_ref, xbuf_ref, a_ref, u_ref, h_ref, *, n_blocks):
    tc = rx_ref.shape[1]

    @pl.when(pl.program_id(1) == 0)
    def _():
        xbuf_ref[0:SUBLANES, :] = jnp.zeros((SUBLANES, xbuf_ref.shape[1]), F32)
        h_ref[...] = jnp.zeros_like(h_ref)

    xbuf_ref[SUBLANES:SUBLANES + tc, :] = rx_ref[0].astype(F32)
    y = cb_ref[...]
    for kk in range(CONV_WIDTH):
        off = SUBLANES - (CONV_WIDTH - 1) + kk
        y = y + cw_ref[kk:kk + 1, :] * xbuf_ref[off:off + tc, :]
    xbuf_ref[0:SUBLANES, :] = xbuf_ref[tc:tc + SUBLANES, :]

    yb = y.astype(BF16)
    neg_sp = -LRU_C * jax.nn.softplus(-lam_ref[...])
    for n in range(n_blocks):
        sl = slice(n * RNN_BLOCK, (n + 1) * RNN_BLOCK)
        r = jax.nn.sigmoid(jnp.dot(yb[:, sl], wa_ref[n], preferred_element_type=F32) + ba_ref[:, sl])
        g = jax.nn.sigmoid(jnp.dot(yb[:, sl], wi_ref[n], preferred_element_type=F32) + bi_ref[:, sl])
        log_a = r * neg_sp[:, sl]
        a = jnp.exp(log_a)
        a_ref[:, sl] = a
        u_ref[:, sl] = jnp.sqrt(1.0 - a * a) * (g * y[:, sl])

    def scan_body(t, hcur):
        hnew = a_ref[pl.ds(t, 1), :] * hcur + u_ref[pl.ds(t, 1), :]
        u_ref[pl.ds(t, 1), :] = hnew
        return hnew

    h_ref[...] = lax.fori_loop(0, tc, scan_body, h_ref[...], unroll=8)

    o_ref[0] = (jax.nn.gelu(rg_ref[0].astype(F32), approximate=True) * u_ref[...]).astype(o_ref.dtype)


def _rglru(rx, rg, conv_w, conv_b, w_a, b_a, w_i, b_i, lam, *, tc):
    b, s, c = rx.shape
    n_blocks = c // RNN_BLOCK
    seq = lambda bi, i: (bi, i, 0)
    fixed2 = lambda bi, i: (0, 0)
    fixed3 = lambda bi, i: (0, 0, 0)
    return pl.pallas_call(
        functools.partial(_rglru_kernel, n_blocks=n_blocks),
        grid=(b, s // tc),
        in_specs=[
            pl.BlockSpec((1, tc, c), seq),
            pl.BlockSpec((1, tc, c), seq),
            pl.BlockSpec((CONV_WIDTH, c), fixed2),
            pl.BlockSpec((1, c), fixed2),
            pl.BlockSpec((n_blocks, RNN_BLOCK, RNN_BLOCK), fixed3),
            pl.BlockSpec((1, c), fixed2),
            pl.BlockSpec((n_blocks, RNN_BLOCK, RNN_BLOCK), fixed3),
            pl.BlockSpec((1, c), fixed2),
            pl.BlockSpec((1, c), fixed2),
        ],
        out_specs=pl.BlockSpec((1, tc, c), seq),
        out_shape=jax.ShapeDtypeStruct((b, s, c), BF16),
        scratch_shapes=[pltpu.VMEM((tc + SUBLANES, c), F32), pltpu.VMEM((tc, c), F32),
                        pltpu.VMEM((tc, c), F32), pltpu.VMEM((1, c), F32)],
        compiler_params=_params(("arbitrary", "arbitrary")),
        name="rglru",
    )(rx, rg, conv_w, conv_b, w_a, b_a, w_i, b_i, lam)


def _out_proj_kernel(x_ref, att_ref, rnn_ref, wo_ref, g_ref, wrh_ref, wrl_ref,
                     h_ref, hn_ref, route_ref, counts_ref, carry_ref, *, d_att, n_experts):
    @pl.when(pl.program_id(0) == 0)
    def _():
        carry_ref[...] = jnp.zeros_like(carry_ref)

    h = (x_ref[...]
         + jnp.dot(att_ref[...], wo_ref[0:d_att, :], preferred_element_type=F32)
         + jnp.dot(rnn_ref[...], wo_ref[d_att:, :], preferred_element_type=F32))
    h_ref[...] = h
    hn = _rms(h, g_ref[...])
    hn_ref[...] = hn

    hn_hi = hn.astype(BF16)
    hn_lo = (hn - hn_hi.astype(F32)).astype(BF16)
    logits = (jnp.dot(hn_hi, wrh_ref[...], preferred_element_type=F32)
              + (jnp.dot(hn_hi, wrl_ref[...], preferred_element_type=F32)
                 + jnp.dot(hn_lo, wrh_ref[...], preferred_element_type=F32)))
    tm = logits.shape[0]
    lane = lax.broadcasted_iota(jnp.int32, logits.shape, 1)
    neg = -jnp.inf

    def first_argmax(vals, vmax):
        return jnp.min(jnp.where(vals == vmax, lane, LANES), axis=-1, keepdims=True)

    g_mask = (lane >= n_experts) & (lane < n_experts + N_GROUPS)
    g_log = jnp.where(g_mask, logits, neg)
    g_max = jnp.max(g_log, axis=-1, keepdims=True)
    g_w = 1.0 / jnp.sum(jnp.exp(g_log - g_max), axis=-1, keepdims=True)
    g_idx = first_argmax(g_log, g_max) - n_experts

    e_lo = g_idx * EXPERTS_PER_GROUP
    e_log = jnp.where((lane >= e_lo) & (lane < e_lo + EXPERTS_PER_GROUP), logits, neg)
    l1 = jnp.max(e_log, axis=-1, keepdims=True)
    i1 = first_argmax(e_log, l1)
    e_log2 = jnp.where(lane == i1, neg, e_log)
    l2 = jnp.max(e_log2, axis=-1, keepdims=True)
    i2 = first_argmax(e_log2, l2)
    e2 = jnp.exp(l2 - l1)
    w1 = g_w / (1.0 + e2)
    w2 = g_w * e2 / (1.0 + e2)

    hot1 = lane == i1
    hot2 = lane == i2
    hot = jnp.where(hot1 | hot2, 1.0, 0.0)
    before = (lax.broadcasted_iota(jnp.int32, (tm, tm), 0)
              > lax.broadcasted_iota(jnp.int32, (tm, tm), 1)).astype(BF16)
    seen = jnp.dot(before, hot.astype(BF16), preferred_element_type=F32) + carry_ref[...]
    rank1 = jnp.sum(jnp.where(hot1, seen, 0.0), axis=-1, keepdims=True)
    rank2 = jnp.sum(jnp.where(hot2, seen, 0.0), axis=-1, keepdims=True)
    carry_ref[...] += jnp.sum(hot, axis=0, keepdims=True)
    counts_ref[...] = jnp.broadcast_to(carry_ref[...], counts_ref.shape)

    rec = jnp.zeros(logits.shape, F32)
    for lane_idx, col in ((R_E1, i1.astype(F32)), (R_E2, i2.astype(F32)), (R_W1, w1), (R_W2, w2),
                          (R_RANK1, rank1), (R_RANK2, rank2)):
        rec = jnp.where(lane == lane_idx, col, rec)
    route_ref[...] = rec


def _out_proj(x2, att, rnn, w_out, gain, w_router_hi, w_router_lo, *, tm, n_experts):
    t, d = x2.shape
    d_att = att.shape[1]
    d_rnn = rnn.shape[1]
    row = lambda i: (i, 0)
    fixed = lambda i: (0, 0)
    return pl.pallas_call(
        functools.partial(_out_proj_kernel, d_att=d_att, n_experts=n_experts),
        grid=(t // tm,),
        in_specs=[
            pl.BlockSpec((tm, d), row),
            pl.BlockSpec((tm, d_att), row),
            pl.BlockSpec((tm, d_rnn), row),
            pl.BlockSpec((d_att + d_rnn, d), fixed),
            pl.BlockSpec((1, d), fixed),
            pl.BlockSpec((d, LANES), fixed),
            pl.BlockSpec((d, LANES), fixed),
        ],
        out_specs=[pl.BlockSpec((tm, d), row), pl.BlockSpec((tm, d), row),
                   pl.BlockSpec((tm, LANES), row), pl.BlockSpec((SUBLANES, LANES), fixed)],
        out_shape=[jax.ShapeDtypeStruct((t, d), F32), jax.ShapeDtypeStruct((t, d), F32),
                   jax.ShapeDtypeStruct((t, LANES), F32), jax.ShapeDtypeStruct((SUBLANES, LANES), F32)],
        scratch_shapes=[pltpu.VMEM((1, LANES), F32)],
        compiler_params=_params(("arbitrary",)),
        name="out_proj",
    )(x2, att, rnn, w_out, gain, w_router_hi, w_router_lo)


def _row_copy(src_hbm, src_row, dst_ref, dst_row, sem):
    return pltpu.make_async_copy(src_hbm.at[pl.ds(src_row, 1)], dst_ref.at[pl.ds(dst_row, 1)], sem)


def _dispatch_kernel(e1_ref, e2_ref, r1_ref, r2_ref, off_ref, ztile_ref, hn_hbm, xs_hbm,
                     zero_ref, zsem, sem):
    i = pl.program_id(0)
    tm = e1_ref.shape[0]
    tile_rows = zero_ref.shape[0]

    @pl.when(i == 0)
    def _():
        zero_ref[...] = jnp.zeros_like(zero_ref)

        def zero_copy(z):
            start = pl.multiple_of(ztile_ref[z] * tile_rows, tile_rows)
            return pltpu.make_async_copy(zero_ref, xs_hbm.at[pl.ds(start, tile_rows)], zsem)

        for z in range(ztile_ref.shape[0]):
            @pl.when(ztile_ref[z] >= 0)
            def _():
                zero_copy(z).start()
        for z in range(ztile_ref.shape[0]):
            @pl.when(ztile_ref[z] >= 0)
            def _():
                zero_copy(z).wait()

    def issue(t, carry):
        g = i * tm + t
        _row_copy(hn_hbm, g, xs_hbm, off_ref[e1_ref[t]] + r1_ref[t], sem).start()
        _row_copy(hn_hbm, g, xs_hbm, off_ref[e2_ref[t]] + r2_ref[t], sem).start()
        return carry

    lax.fori_loop(0, tm, issue, 0, unroll=DMA_UNROLL)

    def drain(t, carry):
        _row_copy(hn_hbm, 0, xs_hbm, 0, sem).wait()
        return carry

    lax.fori_loop(0, TOP_K * tm, drain, 0, unroll=DMA_UNROLL)


def _moe_dispatch(e1, e2, r1, r2, offsets, ztile, hn, *, n_rows, tile_rows, tm):
    t, d = hn.shape
    n_experts = offsets.shape[0]
    smem_blk = pl.BlockSpec((tm,), lambda i: (i,), memory_space=pltpu.SMEM)
    smem_all = pl.BlockSpec(memory_space=pltpu.SMEM)
    return pl.pallas_call(
        _dispatch_kernel,
        grid=(t // tm,),
        in_specs=[smem_blk, smem_blk, smem_blk, smem_blk, smem_all, smem_all,
                  pl.BlockSpec(memory_space=pl.ANY)],
        out_specs=pl.BlockSpec(memory_space=pl.ANY),
        out_shape=jax.ShapeDtypeStruct((n_rows, d), F32),
        scratch_shapes=[pltpu.VMEM((tile_rows, d), F32), pltpu.SemaphoreType.DMA,
                        pltpu.SemaphoreType.DMA],
        compiler_params=_params(("arbitrary",)),
        name="moe_dispatch",
    )(e1, e2, r1, r2, offsets, ztile, hn)


def _experts_kernel(texp_ref, nused_ref, x_ref, wgu_ref, wd_ref, y_ref):
    d_e = wd_ref.shape[1]

    @pl.when(pl.program_id(0) < nused_ref[0])
    def _():
        x = x_ref[...].astype(BF16)
        gu = jnp.dot(x, wgu_ref[0], preferred_element_type=F32)
        hid = jax.nn.silu(gu[:, :d_e]) * gu[:, d_e:]
        y_ref[...] = jnp.dot(hid.astype(BF16), wd_ref[0], preferred_element_type=F32)

    @pl.when(pl.program_id(0) >= nused_ref[0])
    def _():
        y_ref[...] = jnp.zeros_like(y_ref)


def _moe_experts(tile_expert, n_used, xs, w_gu, w_d, *, tile_rows):
    n_rows, d = xs.shape
    n_tiles = n_rows // tile_rows
    d_e = w_d.shape[1]
    row = lambda i, te, nu: (jnp.minimum(i, nu[0] - 1), 0)
    wmap = lambda i, te, nu: (te[i], 0, 0)
    grid_spec = pltpu.PrefetchScalarGridSpec(
        num_scalar_prefetch=2,
        grid=(n_tiles,),
        in_specs=[pl.BlockSpec((tile_rows, d), row),
                  pl.BlockSpec((1, d, 2 * d_e), wmap),
                  pl.BlockSpec((1, d_e, d), wmap)],
        out_specs=pl.BlockSpec((tile_rows, d), lambda i, te, nu: (i, 0)),
    )
    return pl.pallas_call(
        _experts_kernel,
        grid_spec=grid_spec,
        out_shape=jax.ShapeDtypeStruct((n_rows, d), F32),
        compiler_params=_params(("arbitrary",)),
        name="moe_experts",
    )(tile_expert, n_used, xs, w_gu, w_d)


def _moe_ple_kernel(e1_ref, e2_ref, r1_ref, r2_ref, e1n_ref, e2n_ref, r1n_ref, r2n_ref, off_ref,
                    y_hbm, route_ref, h_ref, p_ref, g_ref, wg_ref, wu_ref, o_ref,
                    buf1_ref, buf2_ref, sem):
    i = pl.program_id(0)
    n = pl.num_programs(0)
    tm = h_ref.shape[0]
    slot = i % 2

    def gather(e1s, e2s, r1s, r2s, slot_idx):
        def issue(t, carry):
            _row_copy(y_hbm, off_ref[e1s[t]] + r1s[t], buf1_ref.at[slot_idx], t, sem.at[slot_idx]).start()
            _row_copy(y_hbm, off_ref[e2s[t]] + r2s[t], buf2_ref.at[slot_idx], t, sem.at[slot_idx]).start()
            return carry
        lax.fori_loop(0, tm, issue, 0, unroll=DMA_UNROLL)

    @pl.when(i == 0)
    def _():
        gather(e1_ref, e2_ref, r1_ref, r2_ref, 0)

    @pl.when(i + 1 < n)
    def _():
        gather(e1n_ref, e2n_ref, r1n_ref, r2n_ref, 1 - slot)

    def drain(t, carry):
        _row_copy(y_hbm, 0, buf1_ref.at[slot], 0, sem.at[slot]).wait()
        return carry

    lax.fori_loop(0, TOP_K * tm, drain, 0, unroll=DMA_UNROLL)

    rec = route_ref[...]
    h = (h_ref[...] + _lane_pick(rec, R_W1) * buf1_ref[slot] + _lane_pick(rec, R_W2) * buf2_ref[slot])
    hn = _rms(h, g_ref[...]).astype(BF16)
    gate = jax.nn.sigmoid(jnp.dot(hn, wg_ref[...], preferred_element_type=F32))
    up = jnp.dot(p_ref[...].astype(BF16), wu_ref[...], preferred_element_type=F32)
    o_ref[...] = h + gate * up


def _moe_ple(e1, e2, r1, r2, offsets, y, route, h, p2, gain, w_gate, w_up, *, tm):
    t, d = h.shape
    d_ple = p2.shape[1]
    n = t // tm
    row = lambda i: (i, 0)
    fixed = lambda i: (0, 0)
    cur = pl.BlockSpec((tm,), lambda i: (i,), memory_space=pltpu.SMEM)
    nxt = pl.BlockSpec((tm,), lambda i: (jnp.minimum(i + 1, n - 1),), memory_space=pltpu.SMEM)
    return pl.pallas_call(
        _moe_ple_kernel,
        grid=(n,),
        in_specs=[cur, cur, cur, cur, nxt, nxt, nxt, nxt,
                  pl.BlockSpec(memory_space=pltpu.SMEM),
                  pl.BlockSpec(memory_space=pl.ANY),
                  pl.BlockSpec((tm, LANES), row),
                  pl.BlockSpec((tm, d), row),
                  pl.BlockSpec((tm, d_ple), row),
                  pl.BlockSpec((1, d), fixed),
                  pl.BlockSpec((d, d), fixed),
                  pl.BlockSpec((d_ple, d), fixed)],
        out_specs=pl.BlockSpec((tm, d), row),
        out_shape=jax.ShapeDtypeStruct((t, d), F32),
        scratch_shapes=[pltpu.VMEM((2, tm, d), F32), pltpu.VMEM((2, tm, d), F32),
                        pltpu.SemaphoreType.DMA((2,))],
        compiler_params=_params(("arbitrary",)),
        name="moe_ple",
    )(e1, e2, r1, r2, e1, e2, r1, r2, offsets, y, route, h, p2, gain, w_gate, w_up)


def _tile(n, want):
    if n <= want:
        return n
    for cand in range(want, 0, -SUBLANES):
        if n % cand == 0:
            return cand
    return n


def kernel(x, p, mix_norm, w_in, b_forget, q_norm, k_norm, conv_w, conv_b, w_rec_gate, b_rec_gate,
           w_in_gate, b_in_gate, lru_lambda, w_out, ffn_norm, w_router_group, w_router_expert,
           w_expert_gate, w_expert_up, w_expert_down, ple_norm, w_ple_gate, w_ple_up):
    b, s, d = x.shape
    depth = w_in.shape[0]
    n_heads = b_forget.shape[1]
    d_att = n_heads * HEAD_DIM
    d_rnn = conv_w.shape[2]
    n_experts = w_router_expert.shape[2]
    t = b * s
    col_f = 3 * d_att
    col_rx = col_f + n_heads

    tm = _tile(t, 512)
    tq = _tile(s, 512)
    tc_lru = _tile(s, 256)
    tm_disp = _tile(t, 1024)
    tm_comb = _tile(t, 256)
    tile_rows = _tile(t, 512)
    n_rows = TOP_K * t + n_experts * tile_rows
    n_tiles = n_rows // tile_rows

    h = x.reshape(t, d)
    for li in range(depth):
        w = w_in[li]
        w_main = jnp.concatenate([w[:, :col_f], w[:, col_rx:]], axis=1).astype(BF16)
        w_f = jnp.pad(w[:, col_f:col_rx], ((0, 0), (0, LANES - n_heads))).astype(BF16)
        scale = HEAD_DIM ** -0.5
        q_gain = (q_norm[li] * scale).reshape(1, HEAD_DIM)
        k_gain = k_norm[li].reshape(1, HEAD_DIM)
        q, k, v, rx, rg, f_logits = _in_proj(
            h, mix_norm[li].reshape(1, d), w_main, w_f, q_gain, k_gain, tm=tm)

        b_pad = jnp.pad(b_forget[li], (0, LANES - n_heads)).reshape(1, LANES)
        ct = _forget_cum(f_logits.reshape(b, s, LANES), b_pad, tc=tq)
        c_first = ct[:, :, :, 0].transpose(0, 2, 1).reshape(-1)
        c_last = ct[:, :, :, tq - 1].transpose(0, 2, 1).reshape(-1)
        qk_bound = (2.0 * 1.02 * HEAD_DIM * scale * jnp.max(jnp.abs(q_norm[li]))
                    * jnp.max(jnp.abs(k_norm[li]))).reshape(1).astype(F32)
        att = _fox_attn(q.reshape(b, s, d_att), k.reshape(b, s, d_att), v.reshape(b, s, d_att),
                        ct, c_first, c_last, qk_bound, tq=tq)

        rnn = _rglru(rx.reshape(b, s, d_rnn), rg.reshape(b, s, d_rnn), conv_w[li],
                     conv_b[li].reshape(1, d_rnn), w_rec_gate[li].astype(BF16),
                     b_rec_gate[li].reshape(1, d_rnn), w_in_gate[li].astype(BF16),
                     b_in_gate[li].reshape(1, d_rnn), lru_lambda[li].reshape(1, d_rnn), tc=tc_lru)

        w_router = jnp.pad(jnp.concatenate([w_router_expert[li], w_router_group[li]], axis=1),
                           ((0, 0), (0, LANES - n_experts - N_GROUPS)))
        w_router_hi = w_router.astype(BF16)
        w_router_lo = (w_router - w_router_hi.astype(F32)).astype(BF16)
        h1, hn, route, counts = _out_proj(h, att.reshape(t, d_att), rnn.reshape(t, d_rnn),
                                          w_out[li].astype(BF16), ffn_norm[li].reshape(1, d),
                                          w_router_hi, w_router_lo, tm=tm, n_experts=n_experts)

        cnt = counts[0, :n_experts].astype(jnp.int32)
        padded = (cnt + tile_rows - 1) // tile_rows * tile_rows
        ends = jnp.cumsum(padded)
        offsets = ends - padded
        n_used = (ends[-1] // tile_rows).reshape(1)
        tile_expert = jnp.minimum(
            jnp.searchsorted(ends, jnp.arange(n_tiles, dtype=jnp.int32) * tile_rows, side="right"),
            n_experts - 1).astype(jnp.int32)
        tile_expert = jnp.where(jnp.arange(n_tiles) < n_used[0], tile_expert, tile_expert[n_used[0] - 1])
        seg_last = jnp.where(padded > 0, ends // tile_rows - 1, -1)
        tail = n_used[0] + jnp.arange(n_tiles - TOP_K * t // tile_rows, dtype=jnp.int32)
        ztile = jnp.concatenate([seg_last, jnp.where(tail < n_tiles, tail, -1)]).astype(jnp.int32)
        e1 = route[:, R_E1].astype(jnp.int32)
        e2 = route[:, R_E2].astype(jnp.int32)
        r1 = route[:, R_RANK1].astype(jnp.int32)
        r2 = route[:, R_RANK2].astype(jnp.int32)

        xs = _moe_dispatch(e1, e2, r1, r2, offsets, ztile, hn,
                           n_rows=n_rows, tile_rows=tile_rows, tm=tm_disp)
        w_gu = jnp.concatenate([w_expert_gate[li], w_expert_up[li]], axis=2).astype(BF16)
        y = _moe_experts(tile_expert, n_used, xs, w_gu, w_expert_down[li].astype(BF16),
                         tile_rows=tile_rows)
        h = _moe_ple(e1, e2, r1, r2, offsets, y, route, h1, p[li].reshape(t, -1),
                     ple_norm[li].reshape(1, d), w_ple_gate[li].astype(BF16),
                     w_ple_up[li].astype(BF16), tm=tm_comb)
    return h.reshape(b, s, d)
```

```python
import functools

import jax
import jax.numpy as jnp
from jax import lax
from jax.experimental import pallas as pl
from jax.experimental.pallas import tpu as pltpu

HEAD_DIM = 128
RNN_BLOCK = 128
CONV_WIDTH = 4
LRU_C = 8.0
N_GROUPS = 4
EXPERTS_PER_GROUP = 8
TOP_K = 2
EPS = 1e-6
LANES = 128
SUBLANES = 8
VMEM_LIMIT = 56 * 1024 * 1024
ZERO_PROB_LOG = 104.0
R_E1, R_E2, R_W1, R_W2, R_RANK1, R_RANK2 = range(6)
DMA_UNROLL = 8

F32 = jnp.float32
BF16 = jnp.bfloat16


def _params(sem):
    return pltpu.CompilerParams(dimension_semantics=sem, vmem_limit_bytes=VMEM_LIMIT)


def _rms(x, gain):
    return x * lax.rsqrt(jnp.mean(x * x, axis=-1, keepdims=True) + EPS) * gain


def _lane_pick(rec, lane_idx):
    lane = lax.broadcasted_iota(jnp.int32, rec.shape, 1)
    return jnp.sum(jnp.where(lane == lane_idx, rec, 0.0), axis=-1, keepdims=True)


def _in_proj_kernel(x_ref, g_ref, w_ref, wf_ref, qg_ref, kg_ref,
                    q_ref, k_ref, v_ref, rx_ref, rg_ref, f_ref, xn_ref, *, n_heads):
    j = pl.program_id(1)

    @pl.when(j == 0)
    def _():
        xn = _rms(x_ref[...], g_ref[...]).astype(BF16)
        xn_ref[...] = xn
        f_ref[...] = jnp.dot(xn, wf_ref[...], preferred_element_type=F32)

    acc = jnp.dot(xn_ref[...], w_ref[...], preferred_element_type=F32)

    def head_norm(gain_ref, out_ref):
        for hd in range(n_heads):
            sl = slice(hd * HEAD_DIM, (hd + 1) * HEAD_DIM)
            out_ref[:, sl] = _rms(acc[:, sl], gain_ref[...]).astype(out_ref.dtype)

    @pl.when(j == 0)
    def _():
        head_norm(qg_ref, q_ref)

    @pl.when(j == 1)
    def _():
        head_norm(kg_ref, k_ref)

    @pl.when(j == 2)
    def _():
        v_ref[...] = acc.astype(v_ref.dtype)

    @pl.when(j == 3)
    def _():
        rx_ref[...] = acc.astype(rx_ref.dtype)

    @pl.when(j == 4)
    def _():
        rg_ref[...] = acc.astype(rg_ref.dtype)


def _in_proj(x2, gain, w_main, w_f, q_gain, k_gain, *, tm):
    t, d = x2.shape
    d_att = w_main.shape[1] // 5
    n_heads = d_att // HEAD_DIM
    row = lambda i, j: (i, 0)
    fixed = lambda i, j: (0, 0)
    big = jax.ShapeDtypeStruct((t, d_att), BF16)
    return pl.pallas_call(
        functools.partial(_in_proj_kernel, n_heads=n_heads),
        grid=(t // tm, 5),
        in_specs=[
            pl.BlockSpec((tm, d), row),
            pl.BlockSpec((1, d), fixed),
            pl.BlockSpec((d, d_att), lambda i, j: (0, j)),
            pl.BlockSpec((d, LANES), fixed),
            pl.BlockSpec((1, HEAD_DIM), fixed),
            pl.BlockSpec((1, HEAD_DIM), fixed),
        ],
        out_specs=[pl.BlockSpec((tm, d_att), row)] * 5 + [pl.BlockSpec((tm, LANES), row)],
        out_shape=[big] * 5 + [jax.ShapeDtypeStruct((t, LANES), F32)],
        scratch_shapes=[pltpu.VMEM((tm, d), BF16)],
        compiler_params=_params(("arbitrary", "arbitrary")),
        name="in_proj",
    )(x2, gain, w_main, w_f, q_gain, k_gain)


def _forget_cum_kernel(f_ref, b_ref, ct_ref, carry_ref):
    @pl.when(pl.program_id(1) == 0)
    def _():
        carry_ref[...] = jnp.zeros_like(carry_ref)

    log_f = jax.nn.log_sigmoid(f_ref[0] + b_ref[...])
    tc = log_f.shape[0]
    tri = (lax.broadcasted_iota(jnp.int32, (tc, tc), 0)
           >= lax.broadcasted_iota(jnp.int32, (tc, tc), 1)).astype(F32)
    c = jnp.dot(tri, log_f, preferred_element_type=F32,
                precision=lax.Precision.HIGHEST) + carry_ref[...]
    ct_ref[0, 0] = c.T[:SUBLANES, :]
    carry_ref[...] = c[tc - 1:tc, :]


def _forget_cum(f_logits, b_pad, *, tc):
    b, s, _ = f_logits.shape
    return pl.pallas_call(
        _forget_cum_kernel,
        grid=(b, s // tc),
        in_specs=[pl.BlockSpec((1, tc, LANES), lambda bi, i: (bi, i, 0)),
                  pl.BlockSpec((1, LANES), lambda bi, i: (0, 0))],
        out_specs=pl.BlockSpec((1, 1, SUBLANES, tc), lambda bi, i: (bi, i, 0, 0)),
        out_shape=jax.ShapeDtypeStruct((b, s // tc, SUBLANES, tc), F32),
        scratch_shapes=[pltpu.VMEM((1, LANES), F32)],
        compiler_params=_params(("arbitrary", "arbitrary")),
        name="forget_cum",
    )(f_logits, b_pad)


def _attn_kernel(cfirst_ref, clast_ref, qkb_ref, q_ref, k_ref, v_ref, ct_ref, o_ref,
                 vaug_ref, m_ref, acc_ref, sa_ref, sb_ref):
    bi, h, i = pl.program_id(0), pl.program_id(1), pl.program_id(2)
    tq = q_ref.shape[1]
    base = (bi * pl.num_programs(1) + h) * pl.num_programs(2)

    @pl.when(i == 0)
    def _():
        vaug_ref[:, :HEAD_DIM] = v_ref[0]
        vaug_ref[:, HEAD_DIM:] = jnp.ones((vaug_ref.shape[0], HEAD_DIM), vaug_ref.dtype)

    c_ref0 = cfirst_ref[base + i]
    keep_below = c_ref0 + qkb_ref[0] + ZERO_PROB_LOG
    j_lo = lax.while_loop(
        lambda j: (j > 0) & (clast_ref[base + jnp.maximum(j - 1, 0)] < keep_below),
        lambda j: j - 1, i)

    m_ref[...] = jnp.full_like(m_ref, -jnp.inf)
    acc_ref[...] = jnp.zeros_like(acc_ref)

    def scores(j, s_ref):
        off = pl.multiple_of(j * tq, tq)
        s = lax.dot_general(q_ref[0], k_ref[0, pl.ds(off, tq), :], (((1,), (1,)), ((), ())),
                            preferred_element_type=F32)
        s_ref[...] = s + (c_ref0 - ct_ref[0, j, pl.ds(h, 1), :])

    def accumulate(j, s_ref, masked):
        off = pl.multiple_of(j * tq, tq)
        s = s_ref[...]
        if masked:
            rows = lax.broadcasted_iota(jnp.int32, s.shape, 0)
            cols = lax.broadcasted_iota(jnp.int32, s.shape, 1)
            s = jnp.where(rows >= cols, s, -jnp.inf)
        m_prev = m_ref[...]
        m_new = jnp.maximum(m_prev, jnp.max(s, axis=-1, keepdims=True))
        alpha = jnp.exp(m_prev - m_new)
        pr = jnp.exp(s - m_new).astype(BF16)
        acc_ref[...] = alpha * acc_ref[...] + jnp.dot(
            pr, vaug_ref[pl.ds(off, tq), :], preferred_element_type=F32)
        m_ref[...] = m_new

    n_off = i - j_lo
    odd = n_off % 2
    scores(j_lo, sa_ref)

    @pl.when(odd == 1)
    def _():
        accumulate(j_lo, sa_ref, False)
        scores(j_lo + 1, sa_ref)

    j_even = j_lo + odd

    def pair(kk, carry):
        j = j_even + 2 * kk
        scores(j + 1, sb_ref)
        accumulate(j, sa_ref, False)
        scores(j + 2, sa_ref)
        accumulate(j + 1, sb_ref, False)
        return carry

    lax.fori_loop(0, n_off // 2, pair, 0)
    accumulate(i, sa_ref, True)
    o_ref[0] = (acc_ref[:, :HEAD_DIM] / acc_ref[:, HEAD_DIM:]).astype(o_ref.dtype)


def _fox_attn(q, k, v, ct, c_first, c_last, qk_bound, *, tq):
    b, s, d_att = q.shape
    n_heads = d_att // HEAD_DIM
    nq = s // tq
    q_map = lambda bi, h, i, *_: (bi, i, h)
    kv_map = lambda bi, h, i, *_: (bi, 0, h)
    grid_spec = pltpu.PrefetchScalarGridSpec(
        num_scalar_prefetch=3,
        grid=(b, n_heads, nq),
        in_specs=[
            pl.BlockSpec((1, tq, HEAD_DIM), q_map),
            pl.BlockSpec((1, s, HEAD_DIM), kv_map),
            pl.BlockSpec((1, s, HEAD_DIM), kv_map),
            pl.BlockSpec((1, nq, SUBLANES, tq), lambda bi, h, i, *_: (bi, 0, 0, 0)),
        ],
        out_specs=pl.BlockSpec((1, tq, HEAD_DIM), q_map),
        scratch_shapes=[pltpu.VMEM((s, 2 * HEAD_DIM), BF16), pltpu.VMEM((tq, 1), F32),
                        pltpu.VMEM((tq, 2 * HEAD_DIM), F32), pltpu.VMEM((tq, tq), F32),
                        pltpu.VMEM((tq, tq), F32)],
    )
    return pl.pallas_call(
        _attn_kernel,
        grid_spec=grid_spec,
        out_shape=jax.ShapeDtypeStruct((b, s, d_att), BF16),
        compiler_params=_params(("arbitrary", "arbitrary", "arbitrary")),
        name="fox_attn",
    )(c_first, c_last, qk_bound, q, k, v, ct)


def _rglru_kernel(rx_ref, rg_ref, cw_ref, cb_ref, wa_ref, ba_ref, wi_ref, bi_ref, lam_ref,
                  o_ref, xbuf_ref, a_ref, u_ref, h_ref, *, n_blocks):
    tc = rx_ref.shape[1]

    @pl.when(pl.program_id(1) == 0)
    def _():
        xbuf_ref[0:SUBLANES, :] = jnp.zeros((SUBLANES, xbuf_ref.shape[1]), F32)
        h_ref[...] = jnp.zeros_like(h_ref)

    xbuf_ref[SUBLANES:SUBLANES + tc, :] = rx_ref[0].astype(F32)
    y = cb_ref[...]
    for kk in range(CONV_WIDTH):
        off = SUBLANES - (CONV_WIDTH - 1) + kk
        y = y + cw_ref[kk:kk + 1, :] * xbuf_ref[off:off + tc, :]
    xbuf_ref[0:SUBLANES, :] = xbuf_ref[tc:tc + SUBLANES, :]

    yb = y.astype(BF16)
    neg_sp = -LRU_C * jax.nn.softplus(-lam_ref[...])
    for n in range(n_blocks):
        sl = slice(n * RNN_BLOCK, (n + 1) * RNN_BLOCK)
        r = jax.nn.sigmoid(jnp.dot(yb[:, sl], wa_ref[n], preferred_element_type=F32) + ba_ref[:, sl])
        g = jax.nn.sigmoid(jnp.dot(yb[:, sl], wi_ref[n], preferred_element_type=F32) + bi_ref[:, sl])
        log_a = r * neg_sp[:, sl]
        a = jnp.exp(log_a)
        a_ref[:, sl] = a
        u_ref[:, sl] = jnp.sqrt(1.0 - a * a) * (g * y[:, sl])

    def scan_body(t, hcur):
        hnew = a_ref[pl.ds(t, 1), :] * hcur + u_ref[pl.ds(t, 1), :]
        u_ref[pl.ds(t, 1), :] = hnew
        return hnew

    h_ref[...] = lax.fori_loop(0, tc, scan_body, h_ref[...], unroll=8)

    o_ref[0] = (jax.nn.gelu(rg_ref[0].astype(F32), approximate=True) * u_ref[...]).astype(o_ref.dtype)


def _rglru(rx, rg, conv_w, conv_b, w_a, b_a, w_i, b_i, lam, *, tc):
    b, s, c = rx.shape
    n_blocks = c // RNN_BLOCK
    seq = lambda bi, i: (bi, i, 0)
    fixed2 = lambda bi, i: (0, 0)
    fixed3 = lambda bi, i: (0, 0, 0)
    return pl.pallas_call(
        functools.partial(_rglru_kernel, n_blocks=n_blocks),
        grid=(b, s // tc),
        in_specs=[
            pl.BlockSpec((1, tc, c), seq),
            pl.BlockSpec((1, tc, c), seq),
            pl.BlockSpec((CONV_WIDTH, c), fixed2),
            pl.BlockSpec((1, c), fixed2),
            pl.BlockSpec((n_blocks, RNN_BLOCK, RNN_BLOCK), fixed3),
            pl.BlockSpec((1, c), fixed2),
            pl.BlockSpec((n_blocks, RNN_BLOCK, RNN_BLOCK), fixed3),
            pl.BlockSpec((1, c), fixed2),
            pl.BlockSpec((1, c), fixed2),
        ],
        out_specs=pl.BlockSpec((1, tc, c), seq),
        out_shape=jax.ShapeDtypeStruct((b, s, c), BF16),
        scratch_shapes=[pltpu.VMEM((tc + SUBLANES, c), F32), pltpu.VMEM((tc, c), F32),
                        pltpu.VMEM((tc, c), F32), pltpu.VMEM((1, c), F32)],
        compiler_params=_params(("arbitrary", "arbitrary")),
        name="rglru",
    )(rx, rg, conv_w, conv_b, w_a, b_a, w_i, b_i, lam)


def _out_proj_kernel(x_ref, att_ref, rnn_ref, wo_ref, g_ref, wrh_ref, wrl_ref,
                     h_ref, hn_ref, route_ref, counts_ref, carry_ref, *, d_att, n_experts):
    @pl.when(pl.program_id(0) == 0)
    def _():
        carry_ref[...] = jnp.zeros_like(carry_ref)

    h = (x_ref[...]
         + jnp.dot(att_ref[...], wo_ref[0:d_att, :], preferred_element_type=F32)
         + jnp.dot(rnn_ref[...], wo_ref[d_att:, :], preferred_element_type=F32))
    h_ref[...] = h
    hn = _rms(h, g_ref[...])
    hn_ref[...] = hn

    hn_hi = hn.astype(BF16)
    hn_lo = (hn - hn_hi.astype(F32)).astype(BF16)
    logits = (jnp.dot(hn_hi, wrh_ref[...], preferred_element_type=F32)
              + (jnp.dot(hn_hi, wrl_ref[...], preferred_element_type=F32)
                 + jnp.dot(hn_lo, wrh_ref[...], preferred_element_type=F32)))
    tm = logits.shape[0]
    lane = lax.broadcasted_iota(jnp.int32, logits.shape, 1)
    neg = -jnp.inf

    def first_argmax(vals, vmax):
        return jnp.min(jnp.where(vals == vmax, lane, LANES), axis=-1, keepdims=True)

    g_mask = (lane >= n_experts) & (lane < n_experts + N_GROUPS)
    g_log = jnp.where(g_mask, logits, neg)
    g_max = jnp.max(g_log, axis=-1, keepdims=True)
    g_w = 1.0 / jnp.sum(jnp.exp(g_log - g_max), axis=-1, keepdims=True)
    g_idx = first_argmax(g_log, g_max) - n_experts

    e_lo = g_idx * EXPERTS_PER_GROUP
    e_log = jnp.where((lane >= e_lo) & (lane < e_lo + EXPERTS_PER_GROUP), logits, neg)
    l1 = jnp.max(e_log, axis=-1, keepdims=True)
    i1 = first_argmax(e_log, l1)
    e_log2 = jnp.where(lane == i1, neg, e_log)
    l2 = jnp.max(e_log2, axis=-1, keepdims=True)
    i2 = first_argmax(e_log2, l2)
    e2 = jnp.exp(l2 - l1)
    w1 = g_w / (1.0 + e2)
    w2 = g_w * e2 / (1.0 + e2)

    hot1 = lane == i1
    hot2 = lane == i2
    hot = jnp.where(hot1 | hot2, 1.0, 0.0)
    before = (lax.broadcasted_iota(jnp.int32, (tm, tm), 0)
              > lax.broadcasted_iota(jnp.int32, (tm, tm), 1)).astype(BF16)
    seen = jnp.dot(before, hot.astype(BF16), preferred_element_type=F32) + carry_ref[...]
    rank1 = jnp.sum(jnp.where(hot1, seen, 0.0), axis=-1, keepdims=True)
    rank2 = jnp.sum(jnp.where(hot2, seen, 0.0), axis=-1, keepdims=True)
    carry_ref[...] += jnp.sum(hot, axis=0, keepdims=True)
    counts_ref[...] = jnp.broadcast_to(carry_ref[...], counts_ref.shape)

    rec = jnp.zeros(logits.shape, F32)
    for lane_idx, col in ((R_E1, i1.astype(F32)), (R_E2, i2.astype(F32)), (R_W1, w1), (R_W2, w2),
                          (R_RANK1, rank1), (R_RANK2, rank2)):
        rec = jnp.where(lane == lane_idx, col, rec)
    route_ref[...] = rec


def _out_proj(x2, att, rnn, w_out, gain, w_router_hi, w_router_lo, *, tm, n_experts):
    t, d = x2.shape
    d_att = att.shape[1]
    d_rnn = rnn.shape[1]
    row = lambda i: (i, 0)
    fixed = lambda i: (0, 0)
    return pl.pallas_call(
        functools.partial(_out_proj_kernel, d_att=d_att, n_experts=n_experts),
        grid=(t // tm,),
        in_specs=[
            pl.BlockSpec((tm, d), row),
            pl.BlockSpec((tm, d_att), row),
            pl.BlockSpec((tm, d_rnn), row),
            pl.BlockSpec((d_att + d_rnn, d), fixed),
            pl.BlockSpec((1, d), fixed),
            pl.BlockSpec((d, LANES), fixed),
            pl.BlockSpec((d, LANES), fixed),
        ],
        out_specs=[pl.BlockSpec((tm, d), row), pl.BlockSpec((tm, d), row),
                   pl.BlockSpec((tm, LANES), row), pl.BlockSpec((SUBLANES, LANES), fixed)],
        out_shape=[jax.ShapeDtypeStruct((t, d), F32), jax.ShapeDtypeStruct((t, d), F32),
                   jax.ShapeDtypeStruct((t, LANES), F32), jax.ShapeDtypeStruct((SUBLANES, LANES), F32)],
        scratch_shapes=[pltpu.VMEM((1, LANES), F32)],
        compiler_params=_params(("arbitrary",)),
        name="out_proj",
    )(x2, att, rnn, w_out, gain, w_router_hi, w_router_lo)


def _row_copy(src_hbm, src_row, dst_ref, dst_row, sem):
    return pltpu.make_async_copy(src_hbm.at[pl.ds(src_row, 1)], dst_ref.at[pl.ds(dst_row, 1)], sem)


def _dispatch_kernel(p1_ref, p2_ref, ztile_ref, hn_ref, xs_hbm, zero_ref, zsem, sem):
    i = pl.program_id(0)
    tm = hn_ref.shape[0]
    tile_rows = zero_ref.shape[0]

    @pl.when(i == 0)
    def _():
        zero_ref[...] = jnp.zeros_like(zero_ref)

        def zero_copy(z):
            start = pl.multiple_of(ztile_ref[z] * tile_rows, tile_rows)
            return pltpu.make_async_copy(zero_ref, xs_hbm.at[pl.ds(start, tile_rows)], zsem)

        for z in range(ztile_ref.shape[0]):
            @pl.when(ztile_ref[z] >= 0)
            def _():
                zero_copy(z).start()
        for z in range(ztile_ref.shape[0]):
            @pl.when(ztile_ref[z] >= 0)
            def _():
                zero_copy(z).wait()

    def issue(t, carry):
        _row_copy(hn_ref, t, xs_hbm, p1_ref[t], sem).start()
        _row_copy(hn_ref, t, xs_hbm, p2_ref[t], sem).start()
        return carry

    lax.fori_loop(0, tm, issue, 0, unroll=DMA_UNROLL)

    def drain(t, carry):
        _row_copy(hn_ref, 0, xs_hbm, 0, sem).wait()
        return carry

    lax.fori_loop(0, TOP_K * tm, drain, 0, unroll=DMA_UNROLL)


def _moe_dispatch(pos1, pos2, ztile, hn, *, n_rows, tile_rows, tm):
    t, d = hn.shape
    smem_blk = pl.BlockSpec((tm,), lambda i: (i,), memory_space=pltpu.SMEM)
    return pl.pallas_call(
        _dispatch_kernel,
        grid=(t // tm,),
        in_specs=[smem_blk, smem_blk, pl.BlockSpec(memory_space=pltpu.SMEM),
                  pl.BlockSpec((tm, d), lambda i: (i, 0))],
        out_specs=pl.BlockSpec(memory_space=pl.ANY),
        out_shape=jax.ShapeDtypeStruct((n_rows, d), F32),
        scratch_shapes=[pltpu.VMEM((tile_rows, d), F32), pltpu.SemaphoreType.DMA,
                        pltpu.SemaphoreType.DMA],
        compiler_params=_params(("arbitrary",)),
        name="moe_dispatch",
    )(pos1, pos2, ztile, hn)


def _experts_kernel(texp_ref, nused_ref, x_ref, wgu_ref, wd_ref, y_ref):
    d_e = wd_ref.shape[1]

    @pl.when(pl.program_id(0) < nused_ref[0])
    def _():
        x = x_ref[...].astype(BF16)
        gu = jnp.dot(x, wgu_ref[0], preferred_element_type=F32)
        hid = jax.nn.silu(gu[:, :d_e]) * gu[:, d_e:]
        y_ref[...] = jnp.dot(hid.astype(BF16), wd_ref[0], preferred_element_type=F32)

    @pl.when(pl.program_id(0) >= nused_ref[0])
    def _():
        y_ref[...] = jnp.zeros_like(y_ref)


def _moe_experts(tile_expert, n_used, xs, w_gu, w_d, *, tile_rows):
    n_rows, d = xs.shape
    n_tiles = n_rows // tile_rows
    d_e = w_d.shape[1]
    row = lambda i, te, nu: (jnp.minimum(i, nu[0] - 1), 0)
    wmap = lambda i, te, nu: (te[i], 0, 0)
    grid_spec = pltpu.PrefetchScalarGridSpec(
        num_scalar_prefetch=2,
        grid=(n_tiles,),
        in_specs=[pl.BlockSpec((tile_rows, d), row),
                  pl.BlockSpec((1, d, 2 * d_e), wmap),
                  pl.BlockSpec((1, d_e, d), wmap)],
        out_specs=pl.BlockSpec((tile_rows, d), lambda i, te, nu: (i, 0)),
    )
    return pl.pallas_call(
        _experts_kernel,
        grid_spec=grid_spec,
        out_shape=jax.ShapeDtypeStruct((n_rows, d), F32),
        compiler_params=_params(("arbitrary",)),
        name="moe_experts",
    )(tile_expert, n_used, xs, w_gu, w_d)


def _moe_ple_kernel(p1_ref, p2_ref, p1n_ref, p2n_ref,
                    y_hbm, route_ref, h_ref, p_ref, g_ref, wg_ref, wu_ref, o_ref,
                    buf1_ref, buf2_ref, sem):
    i = pl.program_id(0)
    n = pl.num_programs(0)
    tm = h_ref.shape[0]
    slot = i % 2

    def gather(p1s, p2s, slot_idx):
        def issue(t, carry):
            _row_copy(y_hbm, p1s[t], buf1_ref.at[slot_idx], t, sem.at[slot_idx]).start()
            _row_copy(y_hbm, p2s[t], buf2_ref.at[slot_idx], t, sem.at[slot_idx]).start()
            return carry
        lax.fori_loop(0, tm, issue, 0, unroll=DMA_UNROLL)

    @pl.when(i == 0)
    def _():
        gather(p1_ref, p2_ref, 0)

    @pl.when(i + 1 < n)
    def _():
        gather(p1n_ref, p2n_ref, 1 - slot)

    def drain(t, carry):
        _row_copy(y_hbm, 0, buf1_ref.at[slot], 0, sem.at[slot]).wait()
        return carry

    lax.fori_loop(0, TOP_K * tm, drain, 0, unroll=DMA_UNROLL)

    rec = route_ref[...]
    h = (h_ref[...] + _lane_pick(rec, R_W1) * buf1_ref[slot] + _lane_pick(rec, R_W2) * buf2_ref[slot])
    hn = _rms(h, g_ref[...]).astype(BF16)
    gate = jax.nn.sigmoid(jnp.dot(hn, wg_ref[...], preferred_element_type=F32))
    up = jnp.dot(p_ref[...].astype(BF16), wu_ref[...], preferred_element_type=F32)
    o_ref[...] = h + gate * up


def _moe_ple(pos1, pos2, y, route, h, p2, gain, w_gate, w_up, *, tm):
    t, d = h.shape
    d_ple = p2.shape[1]
    n = t // tm
    row = lambda i: (i, 0)
    fixed = lambda i: (0, 0)
    cur = pl.BlockSpec((tm,), lambda i: (i,), memory_space=pltpu.SMEM)
    nxt = pl.BlockSpec((tm,), lambda i: (jnp.minimum(i + 1, n - 1),), memory_space=pltpu.SMEM)
    return pl.pallas_call(
        _moe_ple_kernel,
        grid=(n,),
        in_specs=[cur, cur, nxt, nxt,
                  pl.BlockSpec(memory_space=pl.ANY),
                  pl.BlockSpec((tm, LANES), row),
                  pl.BlockSpec((tm, d), row),
                  pl.BlockSpec((tm, d_ple), row),
                  pl.BlockSpec((1, d), fixed),
                  pl.BlockSpec((d, d), fixed),
                  pl.BlockSpec((d_ple, d), fixed)],
        out_specs=pl.BlockSpec((tm, d), row),
        out_shape=jax.ShapeDtypeStruct((t, d), F32),
        scratch_shapes=[pltpu.VMEM((2, tm, d), F32), pltpu.VMEM((2, tm, d), F32),
                        pltpu.SemaphoreType.DMA((2,))],
        compiler_params=_params(("arbitrary",)),
        name="moe_ple",
    )(pos1, pos2, pos1, pos2, y, route, h, p2, gain, w_gate, w_up)


def _tile(n, want):
    if n <= want:
        return n
    for cand in range(want, 0, -SUBLANES):
        if n % cand == 0:
            return cand
    return n


def kernel(x, p, mix_norm, w_in, b_forget, q_norm, k_norm, conv_w, conv_b, w_rec_gate, b_rec_gate,
           w_in_gate, b_in_gate, lru_lambda, w_out, ffn_norm, w_router_group, w_router_expert,
           w_expert_gate, w_expert_up, w_expert_down, ple_norm, w_ple_gate, w_ple_up):
    b, s, d = x.shape
    depth = w_in.shape[0]
    n_heads = b_forget.shape[1]
    d_att = n_heads * HEAD_DIM
    d_rnn = conv_w.shape[2]
    n_experts = w_router_expert.shape[2]
    t = b * s
    col_f = 3 * d_att
    col_rx = col_f + n_heads

    tm = _tile(t, 512)
    tq = _tile(s, 512)
    tc_lru = _tile(s, 256)
    tm_disp = _tile(t, 1024)
    tm_comb = _tile(t, 256)
    tile_rows = _tile(t, 512)
    n_rows = TOP_K * t + n_experts * tile_rows
    n_tiles = n_rows // tile_rows

    h = x.reshape(t, d)
    for li in range(depth):
        w = w_in[li]
        w_main = jnp.concatenate([w[:, :col_f], w[:, col_rx:]], axis=1).astype(BF16)
        w_f = jnp.pad(w[:, col_f:col_rx], ((0, 0), (0, LANES - n_heads))).astype(BF16)
        scale = HEAD_DIM ** -0.5
        q_gain = (q_norm[li] * scale).reshape(1, HEAD_DIM)
        k_gain = k_norm[li].reshape(1, HEAD_DIM)
        q, k, v, rx, rg, f_logits = _in_proj(
            h, mix_norm[li].reshape(1, d), w_main, w_f, q_gain, k_gain, tm=tm)

        b_pad = jnp.pad(b_forget[li], (0, LANES - n_heads)).reshape(1, LANES)
        ct = _forget_cum(f_logits.reshape(b, s, LANES), b_pad, tc=tq)
        c_first = ct[:, :, :, 0].transpose(0, 2, 1).reshape(-1)
        c_last = ct[:, :, :, tq - 1].transpose(0, 2, 1).reshape(-1)
        qk_bound = (2.0 * 1.02 * HEAD_DIM * scale * jnp.max(jnp.abs(q_norm[li]))
                    * jnp.max(jnp.abs(k_norm[li]))).reshape(1).astype(F32)
        att = _fox_attn(q.reshape(b, s, d_att), k.reshape(b, s, d_att), v.reshape(b, s, d_att),
                        ct, c_first, c_last, qk_bound, tq=tq)

        rnn = _rglru(rx.reshape(b, s, d_rnn), rg.reshape(b, s, d_rnn), conv_w[li],
                     conv_b[li].reshape(1, d_rnn), w_rec_gate[li].astype(BF16),
                     b_rec_gate[li].reshape(1, d_rnn), w_in_gate[li].astype(BF16),
                     b_in_gate[li].reshape(1, d_rnn), lru_lambda[li].reshape(1, d_rnn), tc=tc_lru)

        w_router = jnp.pad(jnp.concatenate([w_router_expert[li], w_router_group[li]], axis=1),
                           ((0, 0), (0, LANES - n_experts - N_GROUPS)))
        w_router_hi = w_router.astype(BF16)
        w_router_lo = (w_router - w_router_hi.astype(F32)).astype(BF16)
        h1, hn, route, counts = _out_proj(h, att.reshape(t, d_att), rnn.reshape(t, d_rnn),
                                          w_out[li].astype(BF16), ffn_norm[li].reshape(1, d),
                                          w_router_hi, w_router_lo, tm=tm, n_experts=n_experts)

        cnt = counts[0, :n_experts].astype(jnp.int32)
        padded = (cnt + tile_rows - 1) // tile_rows * tile_rows
        ends = jnp.cumsum(padded)
        offsets = ends - padded
        n_used = (ends[-1] // tile_rows).reshape(1)
        tile_start = jnp.minimum(jnp.arange(n_tiles, dtype=jnp.int32), n_used[0] - 1) * tile_rows
        tile_expert = jnp.sum(ends[None, :] <= tile_start[:, None], axis=1).astype(jnp.int32)
        seg_last = jnp.where(padded > 0, ends // tile_rows - 1, -1)
        tail = n_used[0] + jnp.arange(n_tiles - TOP_K * t // tile_rows, dtype=jnp.int32)
        ztile = jnp.concatenate([seg_last, jnp.where(tail < n_tiles, tail, -1)]).astype(jnp.int32)

        def slot_row(expert_lane, rank_lane):
            e = route[:, expert_lane].astype(jnp.int32)
            hot = e[:, None] == jnp.arange(n_experts, dtype=jnp.int32)[None, :]
            return route[:, rank_lane].astype(jnp.int32) + jnp.sum(jnp.where(hot, offsets[None, :], 0), axis=1)

        pos1 = slot_row(R_E1, R_RANK1)
        pos2 = slot_row(R_E2, R_RANK2)

        xs = _moe_dispatch(pos1, pos2, ztile, hn, n_rows=n_rows, tile_rows=tile_rows, tm=tm_disp)
        w_gu = jnp.concatenate([w_expert_gate[li], w_expert_up[li]], axis=2).astype(BF16)
        y = _moe_experts(tile_expert, n_used, xs, w_gu, w_expert_down[li].astype(BF16),
                         tile_rows=tile_rows)
        h = _moe_ple(pos1, pos2, y, route, h1, p[li].reshape(t, -1),
                     ple_norm[li].reshape(1, d), w_ple_gate[li].astype(BF16),
                     w_ple_up[li].astype(BF16), tm=tm_comb)
    return h.reshape(b, s, d)
```

```python
import functools

import jax
import jax.numpy as jnp
from jax import lax
from jax.experimental import pallas as pl
from jax.experimental.pallas import tpu as pltpu

HEAD_DIM = 128
RNN_BLOCK = 128
CONV_WIDTH = 4
LRU_C = 8.0
N_GROUPS = 4
EXPERTS_PER_GROUP = 8
TOP_K = 2
EPS = 1e-6
LANES = 128
SUBLANES = 8
VMEM_LIMIT = 56 * 1024 * 1024
ZERO_PROB_LOG = 104.0
R_E1, R_E2, R_W1, R_W2, R_RANK1, R_RANK2 = range(6)
DMA_UNROLL = 8

F32 = jnp.float32
BF16 = jnp.bfloat16


def _params(sem):
    return pltpu.CompilerParams(dimension_semantics=sem, vmem_limit_bytes=VMEM_LIMIT)


def _rms(x, gain):
    return x * lax.rsqrt(jnp.mean(x * x, axis=-1, keepdims=True) + EPS) * gain


def _lane_pick(rec, lane_idx):
    lane = lax.broadcasted_iota(jnp.int32, rec.shape, 1)
    return jnp.sum(jnp.where(lane == lane_idx, rec, 0.0), axis=-1, keepdims=True)


def _pack_rows(v):
    half = v.shape[1] // 2
    lo = lax.bitcast_convert_type(v[:, :half].astype(BF16).astype(F32), jnp.uint32)
    hi = lax.bitcast_convert_type(v[:, half:].astype(BF16).astype(F32), jnp.uint32)
    return (lo >> 16) | (hi & jnp.uint32(0xFFFF0000))


def _unpack_rows(w):
    return (lax.bitcast_convert_type(w << 16, F32),
            lax.bitcast_convert_type(w & jnp.uint32(0xFFFF0000), F32))


def _in_proj_kernel(x_ref, g_ref, w_ref, wf_ref, qg_ref, kg_ref, o_ref, f_ref, xn_ref, *, n_heads):
    j = pl.program_id(1)

    @pl.when(j == 0)
    def _():
        xn = _rms(x_ref[...], g_ref[...]).astype(BF16)
        xn_ref[...] = xn
        f_ref[...] = jnp.dot(xn, wf_ref[...], preferred_element_type=F32)

    acc = jnp.dot(xn_ref[...], w_ref[...], preferred_element_type=F32)

    def head_norm(gain_ref):
        for hd in range(n_heads):
            sl = slice(hd * HEAD_DIM, (hd + 1) * HEAD_DIM)
            o_ref[:, sl] = _rms(acc[:, sl], gain_ref[...]).astype(o_ref.dtype)

    @pl.when(j == 0)
    def _():
        head_norm(qg_ref)

    @pl.when(j == 1)
    def _():
        head_norm(kg_ref)

    @pl.when(j >= 2)
    def _():
        o_ref[...] = acc.astype(o_ref.dtype)


def _in_proj(x2, gain, w_main, w_f, q_gain, k_gain, *, tm):
    t, d = x2.shape
    n_col = 5
    d_att = w_main.shape[1] // n_col
    n_heads = d_att // HEAD_DIM
    row = lambda i, j: (i, 0)
    fixed = lambda i, j: (0, 0)
    return pl.pallas_call(
        functools.partial(_in_proj_kernel, n_heads=n_heads),
        grid=(t // tm, n_col),
        in_specs=[
            pl.BlockSpec((tm, d), row),
            pl.BlockSpec((1, d), fixed),
            pl.BlockSpec((d, d_att), lambda i, j: (0, j)),
            pl.BlockSpec((d, LANES), fixed),
            pl.BlockSpec((1, HEAD_DIM), fixed),
            pl.BlockSpec((1, HEAD_DIM), fixed),
        ],
        out_specs=[pl.BlockSpec((tm, d_att), lambda i, j: (i, j)), pl.BlockSpec((tm, LANES), row)],
        out_shape=[jax.ShapeDtypeStruct((t, n_col * d_att), BF16),
                   jax.ShapeDtypeStruct((t, LANES), F32)],
        scratch_shapes=[pltpu.VMEM((tm, d), BF16)],
        compiler_params=_params(("arbitrary", "arbitrary")),
        name="in_proj",
    )(x2, gain, w_main, w_f, q_gain, k_gain)


def _forget_cum_kernel(f_ref, b_ref, ct_ref, carry_ref):
    @pl.when(pl.program_id(1) == 0)
    def _():
        carry_ref[...] = jnp.zeros_like(carry_ref)

    log_f = jax.nn.log_sigmoid(f_ref[0] + b_ref[...])
    tc = log_f.shape[0]
    tri = (lax.broadcasted_iota(jnp.int32, (tc, tc), 0)
           >= lax.broadcasted_iota(jnp.int32, (tc, tc), 1)).astype(F32)
    c = jnp.dot(tri, log_f, preferred_element_type=F32,
                precision=lax.Precision.HIGHEST) + carry_ref[...]
    ct_ref[0, 0] = c.T[:SUBLANES, :]
    carry_ref[...] = c[tc - 1:tc, :]


def _forget_cum(f_logits, b_pad, *, tc):
    b, s, _ = f_logits.shape
    return pl.pallas_call(
        _forget_cum_kernel,
        grid=(b, s // tc),
        in_specs=[pl.BlockSpec((1, tc, LANES), lambda bi, i: (bi, i, 0)),
                  pl.BlockSpec((1, LANES), lambda bi, i: (0, 0))],
        out_specs=pl.BlockSpec((1, 1, SUBLANES, tc), lambda bi, i: (bi, i, 0, 0)),
        out_shape=jax.ShapeDtypeStruct((b, s // tc, SUBLANES, tc), F32),
        scratch_shapes=[pltpu.VMEM((1, LANES), F32)],
        compiler_params=_params(("arbitrary", "arbitrary")),
        name="forget_cum",
    )(f_logits, b_pad)


def _attn_kernel(cfirst_ref, clast_ref, qkb_ref, q_ref, k_ref, v_ref, ct_ref, o_ref,
                 vaug_ref, m_ref, acc_ref, sa_ref, sb_ref):
    bi, h, i = pl.program_id(0), pl.program_id(1), pl.program_id(2)
    tq = q_ref.shape[1]
    base = (bi * pl.num_programs(1) + h) * pl.num_programs(2)

    @pl.when(i == 0)
    def _():
        vaug_ref[:, :HEAD_DIM] = v_ref[0]
        vaug_ref[:, HEAD_DIM:] = jnp.ones((vaug_ref.shape[0], HEAD_DIM), vaug_ref.dtype)

    c_ref0 = cfirst_ref[base + i]
    keep_below = c_ref0 + qkb_ref[0] + ZERO_PROB_LOG
    j_lo = lax.while_loop(
        lambda j: (j > 0) & (clast_ref[base + jnp.maximum(j - 1, 0)] < keep_below),
        lambda j: j - 1, i)

    m_ref[...] = jnp.full_like(m_ref, -jnp.inf)
    acc_ref[...] = jnp.zeros_like(acc_ref)

    def scores(j, s_ref):
        off = pl.multiple_of(j * tq, tq)
        s = lax.dot_general(q_ref[0], k_ref[0, pl.ds(off, tq), :], (((1,), (1,)), ((), ())),
                            preferred_element_type=F32)
        s_ref[...] = s + (c_ref0 - ct_ref[0, j, pl.ds(h, 1), :])

    def accumulate(j, s_ref, masked):
        off = pl.multiple_of(j * tq, tq)
        s = s_ref[...]
        if masked:
            rows = lax.broadcasted_iota(jnp.int32, s.shape, 0)
            cols = lax.broadcasted_iota(jnp.int32, s.shape, 1)
            s = jnp.where(rows >= cols, s, -jnp.inf)
        m_prev = m_ref[...]
        m_new = jnp.maximum(m_prev, jnp.max(s, axis=-1, keepdims=True))
        alpha = jnp.exp(m_prev - m_new)
        pr = jnp.exp(s - m_new).astype(BF16)
        acc_ref[...] = alpha * acc_ref[...] + jnp.dot(
            pr, vaug_ref[pl.ds(off, tq), :], preferred_element_type=F32)
        m_ref[...] = m_new

    n_off = i - j_lo
    odd = n_off % 2
    scores(j_lo, sa_ref)

    @pl.when(odd == 1)
    def _():
        accumulate(j_lo, sa_ref, False)
        scores(j_lo + 1, sa_ref)

    j_even = j_lo + odd

    def pair(kk, carry):
        j = j_even + 2 * kk
        scores(j + 1, sb_ref)
        accumulate(j, sa_ref, False)
        scores(j + 2, sa_ref)
        accumulate(j + 1, sb_ref, False)
        return carry

    lax.fori_loop(0, n_off // 2, pair, 0)
    accumulate(i, sa_ref, True)
    o_ref[0] = (acc_ref[:, :HEAD_DIM] / acc_ref[:, HEAD_DIM:]).astype(o_ref.dtype)


def _fox_attn(proj, ct, c_first, c_last, qk_bound, *, n_heads, tq):
    b, s, _ = proj.shape
    d_att = n_heads * HEAD_DIM
    nq = s // tq
    q_map = lambda bi, h, i, *_: (bi, i, h)
    grid_spec = pltpu.PrefetchScalarGridSpec(
        num_scalar_prefetch=3,
        grid=(b, n_heads, nq),
        in_specs=[
            pl.BlockSpec((1, tq, HEAD_DIM), q_map),
            pl.BlockSpec((1, s, HEAD_DIM), lambda bi, h, i, *_: (bi, 0, n_heads + h)),
            pl.BlockSpec((1, s, HEAD_DIM), lambda bi, h, i, *_: (bi, 0, 2 * n_heads + h)),
            pl.BlockSpec((1, nq, SUBLANES, tq), lambda bi, h, i, *_: (bi, 0, 0, 0)),
        ],
        out_specs=pl.BlockSpec((1, tq, HEAD_DIM), q_map),
        scratch_shapes=[pltpu.VMEM((s, 2 * HEAD_DIM), BF16), pltpu.VMEM((tq, 1), F32),
                        pltpu.VMEM((tq, 2 * HEAD_DIM), F32), pltpu.VMEM((tq, tq), F32),
                        pltpu.VMEM((tq, tq), F32)],
    )
    return pl.pallas_call(
        _attn_kernel,
        grid_spec=grid_spec,
        out_shape=jax.ShapeDtypeStruct((b, s, d_att), BF16),
        compiler_params=_params(("arbitrary", "arbitrary", "arbitrary")),
        name="fox_attn",
    )(c_first, c_last, qk_bound, proj, proj, proj, ct)


def _rglru_kernel(rx_ref, rg_ref, cw_ref, cb_ref, wa_ref, ba_ref, wi_ref, bi_ref, lam_ref,
                  o_ref, xbuf_ref, a_ref, u_ref, h_ref, *, n_blocks):
    tc = rx_ref.shape[1]

    @pl.when(pl.program_id(1) == 0)
    def _():
        xbuf_ref[0:SUBLANES, :] = jnp.zeros((SUBLANES, xbuf_ref.shape[1]), F32)
        h_ref[...] = jnp.zeros_like(h_ref)

    xbuf_ref[SUBLANES:SUBLANES + tc, :] = rx_ref[0].astype(F32)
    y = cb_ref[...]
    for kk in range(CONV_WIDTH):
        off = SUBLANES - (CONV_WIDTH - 1) + kk
        y = y + cw_ref[kk:kk + 1, :] * xbuf_ref[off:off + tc, :]
    xbuf_ref[0:SUBLANES, :] = xbuf_ref[tc:tc + SUBLANES, :]

    yb = y.astype(BF16)
    neg_sp = -LRU_C * jax.nn.softplus(-lam_ref[...])
    for n in range(n_blocks):
        sl = slice(n * RNN_BLOCK, (n + 1) * RNN_BLOCK)
        r = jax.nn.sigmoid(jnp.dot(yb[:, sl], wa_ref[n], preferred_element_type=F32) + ba_ref[:, sl])
        g = jax.nn.sigmoid(jnp.dot(yb[:, sl], wi_ref[n], preferred_element_type=F32) + bi_ref[:, sl])
        log_a = r * neg_sp[:, sl]
        a = jnp.exp(log_a)
        a_ref[:, sl] = a
        u_ref[:, sl] = jnp.sqrt(1.0 - a * a) * (g * y[:, sl])

    def scan_body(t, hcur):
        hnew = a_ref[pl.ds(t, 1), :] * hcur + u_ref[pl.ds(t, 1), :]
        u_ref[pl.ds(t, 1), :] = hnew
        return hnew

    h_ref[...] = lax.fori_loop(0, tc, scan_body, h_ref[...], unroll=8)

    o_ref[0] = (jax.nn.gelu(rg_ref[0].astype(F32), approximate=True) * u_ref[...]).astype(o_ref.dtype)


def _rglru(proj, conv_w, conv_b, w_a, b_a, w_i, b_i, lam, *, tc):
    b, s, _ = proj.shape
    c = conv_w.shape[1]
    n_blocks = c // RNN_BLOCK
    seq = lambda bi, i: (bi, i, 0)
    fixed2 = lambda bi, i: (0, 0)
    fixed3 = lambda bi, i: (0, 0, 0)
    return pl.pallas_call(
        functools.partial(_rglru_kernel, n_blocks=n_blocks),
        grid=(b, s // tc),
        in_specs=[
            pl.BlockSpec((1, tc, c), lambda bi, i: (bi, i, 3)),
            pl.BlockSpec((1, tc, c), lambda bi, i: (bi, i, 4)),
            pl.BlockSpec((CONV_WIDTH, c), fixed2),
            pl.BlockSpec((1, c), fixed2),
            pl.BlockSpec((n_blocks, RNN_BLOCK, RNN_BLOCK), fixed3),
            pl.BlockSpec((1, c), fixed2),
            pl.BlockSpec((n_blocks, RNN_BLOCK, RNN_BLOCK), fixed3),
            pl.BlockSpec((1, c), fixed2),
            pl.BlockSpec((1, c), fixed2),
        ],
        out_specs=pl.BlockSpec((1, tc, c), seq),
        out_shape=jax.ShapeDtypeStruct((b, s, c), BF16),
        scratch_shapes=[pltpu.VMEM((tc + SUBLANES, c), F32), pltpu.VMEM((tc, c), F32),
                        pltpu.VMEM((tc, c), F32), pltpu.VMEM((1, c), F32)],
        compiler_params=_params(("arbitrary", "arbitrary")),
        name="rglru",
    )(proj, proj, conv_w, conv_b, w_a, b_a, w_i, b_i, lam)


def _out_proj_kernel(x_ref, att_ref, rnn_ref, wo_ref, g_ref, wrh_ref, wrl_ref,
                     h_ref, hn_ref, route_ref, counts_ref, carry_ref, *, d_att, n_experts):
    @pl.when(pl.program_id(0) == 0)
    def _():
        carry_ref[...] = jnp.zeros_like(carry_ref)

    h = (x_ref[...]
         + jnp.dot(att_ref[...], wo_ref[0:d_att, :], preferred_element_type=F32)
         + jnp.dot(rnn_ref[...], wo_ref[d_att:, :], preferred_element_type=F32))
    h_ref[...] = h
    hn = _rms(h, g_ref[...])
    hn_ref[...] = _pack_rows(hn)

    hn_hi = hn.astype(BF16)
    hn_lo = (hn - hn_hi.astype(F32)).astype(BF16)
    logits = (jnp.dot(hn_hi, wrh_ref[...], preferred_element_type=F32)
              + (jnp.dot(hn_hi, wrl_ref[...], preferred_element_type=F32)
                 + jnp.dot(hn_lo, wrh_ref[...], preferred_element_type=F32)))
    tm = logits.shape[0]
    lane = lax.broadcasted_iota(jnp.int32, logits.shape, 1)
    neg = -jnp.inf

    def first_argmax(vals, vmax):
        return jnp.min(jnp.where(vals == vmax, lane, LANES), axis=-1, keepdims=True)

    g_mask = (lane >= n_experts) & (lane < n_experts + N_GROUPS)
    g_log = jnp.where(g_mask, logits, neg)
    g_max = jnp.max(g_log, axis=-1, keepdims=True)
    g_w = 1.0 / jnp.sum(jnp.exp(g_log - g_max), axis=-1, keepdims=True)
    g_idx = first_argmax(g_log, g_max) - n_experts

    e_lo = g_idx * EXPERTS_PER_GROUP
    e_log = jnp.where((lane >= e_lo) & (lane < e_lo + EXPERTS_PER_GROUP), logits, neg)
    l1 = jnp.max(e_log, axis=-1, keepdims=True)
    i1 = first_argmax(e_log, l1)
    e_log2 = jnp.where(lane == i1, neg, e_log)
    l2 = jnp.max(e_log2, axis=-1, keepdims=True)
    i2 = first_argmax(e_log2, l2)
    e2 = jnp.exp(l2 - l1)
    w1 = g_w / (1.0 + e2)
    w2 = g_w * e2 / (1.0 + e2)

    hot1 = lane == i1
    hot2 = lane == i2
    hot = jnp.where(hot1 | hot2, 1.0, 0.0)
    before = (lax.broadcasted_iota(jnp.int32, (tm, tm), 0)
              > lax.broadcasted_iota(jnp.int32, (tm, tm), 1)).astype(BF16)
    seen = jnp.dot(before, hot.astype(BF16), preferred_element_type=F32) + carry_ref[...]
    rank1 = jnp.sum(jnp.where(hot1, seen, 0.0), axis=-1, keepdims=True)
    rank2 = jnp.sum(jnp.where(hot2, seen, 0.0), axis=-1, keepdims=True)
    carry_ref[...] += jnp.sum(hot, axis=0, keepdims=True)
    counts_ref[...] = jnp.broadcast_to(carry_ref[...], counts_ref.shape)

    rec = jnp.zeros(logits.shape, F32)
    for lane_idx, col in ((R_E1, i1.astype(F32)), (R_E2, i2.astype(F32)), (R_W1, w1), (R_W2, w2),
                          (R_RANK1, rank1), (R_RANK2, rank2)):
        rec = jnp.where(lane == lane_idx, col, rec)
    route_ref[...] = rec


def _out_proj(x2, att, rnn, w_out, gain, w_router_hi, w_router_lo, *, tm, n_experts):
    t, d = x2.shape
    d_att = att.shape[1]
    d_rnn = rnn.shape[1]
    row = lambda i: (i, 0)
    fixed = lambda i: (0, 0)
    return pl.pallas_call(
        functools.partial(_out_proj_kernel, d_att=d_att, n_experts=n_experts),
        grid=(t // tm,),
        in_specs=[
            pl.BlockSpec((tm, d), row),
            pl.BlockSpec((tm, d_att), row),
            pl.BlockSpec((tm, d_rnn), row),
            pl.BlockSpec((d_att + d_rnn, d), fixed),
            pl.BlockSpec((1, d), fixed),
            pl.BlockSpec((d, LANES), fixed),
            pl.BlockSpec((d, LANES), fixed),
        ],
        out_specs=[pl.BlockSpec((tm, d), row), pl.BlockSpec((tm, d // 2), row),
                   pl.BlockSpec((tm, LANES), row), pl.BlockSpec((SUBLANES, LANES), fixed)],
        out_shape=[jax.ShapeDtypeStruct((t, d), F32), jax.ShapeDtypeStruct((t, d // 2), jnp.uint32),
                   jax.ShapeDtypeStruct((t, LANES), F32), jax.ShapeDtypeStruct((SUBLANES, LANES), F32)],
        scratch_shapes=[pltpu.VMEM((1, LANES), F32)],
        compiler_params=_params(("arbitrary",)),
        name="out_proj",
    )(x2, att, rnn, w_out, gain, w_router_hi, w_router_lo)


def _row_copy(src_hbm, src_row, dst_ref, dst_row, sem):
    return pltpu.make_async_copy(src_hbm.at[pl.ds(src_row, 1)], dst_ref.at[pl.ds(dst_row, 1)], sem)


def _dispatch_kernel(p1_ref, p2_ref, ztile_ref, hn_ref, xs_hbm, zero_ref, zsem, sem):
    i = pl.program_id(0)
    tm = hn_ref.shape[0]
    tile_rows = zero_ref.shape[0]

    @pl.when(i == 0)
    def _():
        zero_ref[...] = jnp.zeros_like(zero_ref)

        def zero_copy(z):
            start = pl.multiple_of(ztile_ref[z] * tile_rows, tile_rows)
            return pltpu.make_async_copy(zero_ref, xs_hbm.at[pl.ds(start, tile_rows)], zsem)

        for z in range(ztile_ref.shape[0]):
            @pl.when(ztile_ref[z] >= 0)
            def _():
                zero_copy(z).start()
        for z in range(ztile_ref.shape[0]):
            @pl.when(ztile_ref[z] >= 0)
            def _():
                zero_copy(z).wait()

    def issue(t, carry):
        _row_copy(hn_ref, t, xs_hbm, p1_ref[t], sem).start(priority=0)
        _row_copy(hn_ref, t, xs_hbm, p2_ref[t], sem).start(priority=1)
        return carry

    lax.fori_loop(0, tm, issue, 0, unroll=DMA_UNROLL)

    def drain(t, carry):
        _row_copy(hn_ref, 0, xs_hbm, 0, sem).wait()
        return carry

    lax.fori_loop(0, TOP_K * tm, drain, 0, unroll=DMA_UNROLL)


def _moe_dispatch(pos1, pos2, ztile, hn, *, n_rows, tile_rows, tm):
    t, width = hn.shape
    smem_blk = pl.BlockSpec((tm,), lambda i: (i,), memory_space=pltpu.SMEM)
    return pl.pallas_call(
        _dispatch_kernel,
        grid=(t // tm,),
        in_specs=[smem_blk, smem_blk, pl.BlockSpec(memory_space=pltpu.SMEM),
                  pl.BlockSpec((tm, width), lambda i: (i, 0))],
        out_specs=pl.BlockSpec(memory_space=pl.ANY),
        out_shape=jax.ShapeDtypeStruct((n_rows, width), hn.dtype),
        scratch_shapes=[pltpu.VMEM((tile_rows, width), hn.dtype), pltpu.SemaphoreType.DMA,
                        pltpu.SemaphoreType.DMA],
        compiler_params=_params(("arbitrary",)),
        name="moe_dispatch",
    )(pos1, pos2, ztile, hn)


def _experts_kernel(texp_ref, nused_ref, x_ref, wg_ref, wu_ref, wd_ref, y_ref, wgu_bf_ref, wd_bf_ref):
    i = pl.program_id(0)
    d_e = wd_ref.shape[1]

    @pl.when((i == 0) | (texp_ref[i] != texp_ref[jnp.maximum(i - 1, 0)]))
    def _():
        wgu_bf_ref[:, :d_e] = wg_ref[0].astype(BF16)
        wgu_bf_ref[:, d_e:] = wu_ref[0].astype(BF16)
        wd_bf_ref[...] = wd_ref[0].astype(BF16)

    @pl.when(i < nused_ref[0])
    def _():
        xa, xb = _unpack_rows(x_ref[...])
        x = jnp.concatenate([xa.astype(BF16), xb.astype(BF16)], axis=1)
        gu = jnp.dot(x, wgu_bf_ref[...], preferred_element_type=F32)
        hid = jax.nn.silu(gu[:, :d_e]) * gu[:, d_e:]
        y_ref[...] = _pack_rows(jnp.dot(hid.astype(BF16), wd_bf_ref[...], preferred_element_type=F32))

    @pl.when(i >= nused_ref[0])
    def _():
        y_ref[...] = jnp.zeros_like(y_ref)


def _moe_experts(tile_expert, n_used, xs, w_gate, w_up, w_down, *, tile_rows):
    n_rows, width = xs.shape
    n_tiles = n_rows // tile_rows
    _, d, d_e = w_gate.shape
    row = lambda i, te, nu: (jnp.minimum(i, nu[0] - 1), 0)
    wmap = lambda i, te, nu: (te[i], 0, 0)
    grid_spec = pltpu.PrefetchScalarGridSpec(
        num_scalar_prefetch=2,
        grid=(n_tiles,),
        in_specs=[pl.BlockSpec((tile_rows, width), row),
                  pl.BlockSpec((1, d, d_e), wmap),
                  pl.BlockSpec((1, d, d_e), wmap),
                  pl.BlockSpec((1, d_e, d), wmap)],
        out_specs=pl.BlockSpec((tile_rows, width), lambda i, te, nu: (i, 0)),
        scratch_shapes=[pltpu.VMEM((d, 2 * d_e), BF16), pltpu.VMEM((d_e, d), BF16)],
    )
    return pl.pallas_call(
        _experts_kernel,
        grid_spec=grid_spec,
        out_shape=jax.ShapeDtypeStruct((n_rows, width), xs.dtype),
        compiler_params=_params(("arbitrary",)),
        name="moe_experts",
    )(tile_expert, n_used, xs, w_gate, w_up, w_down)


def _moe_ple_kernel(p1_ref, p2_ref, p1n_ref, p2n_ref,
                    y_hbm, route_ref, h_ref, p_ref, g_ref, wg_ref, wu_ref, o_ref,
                    buf1_ref, buf2_ref, sem):
    i = pl.program_id(0)
    n = pl.num_programs(0)
    tm = h_ref.shape[0]
    slot = i % 2

    def gather(p1s, p2s, slot_idx):
        def issue(t, carry):
            _row_copy(y_hbm, p1s[t], buf1_ref.at[slot_idx], t, sem.at[slot_idx]).start(priority=0)
            _row_copy(y_hbm, p2s[t], buf2_ref.at[slot_idx], t, sem.at[slot_idx]).start(priority=1)
            return carry
        lax.fori_loop(0, tm, issue, 0, unroll=DMA_UNROLL)

    @pl.when(i == 0)
    def _():
        gather(p1_ref, p2_ref, 0)

    @pl.when(i + 1 < n)
    def _():
        gather(p1n_ref, p2n_ref, 1 - slot)

    def drain(t, carry):
        _row_copy(y_hbm, 0, buf1_ref.at[slot], 0, sem.at[slot]).wait()
        return carry

    lax.fori_loop(0, TOP_K * tm, drain, 0, unroll=DMA_UNROLL)

    rec = route_ref[...]
    w1 = _lane_pick(rec, R_W1)
    w2 = _lane_pick(rec, R_W2)
    y1a, y1b = _unpack_rows(buf1_ref[slot])
    y2a, y2b = _unpack_rows(buf2_ref[slot])
    h = h_ref[...] + jnp.concatenate([w1 * y1a + w2 * y2a, w1 * y1b + w2 * y2b], axis=1)
    hn = _rms(h, g_ref[...]).astype(BF16)
    gate = jax.nn.sigmoid(jnp.dot(hn, wg_ref[...], preferred_element_type=F32))
    up = jnp.dot(p_ref[...].astype(BF16), wu_ref[...], preferred_element_type=F32)
    o_ref[...] = h + gate * up


def _moe_ple(pos1, pos2, y, route, h, p2, gain, w_gate, w_up, *, tm):
    t, d = h.shape
    d_ple = p2.shape[1]
    width = y.shape[1]
    n = t // tm
    row = lambda i: (i, 0)
    fixed = lambda i: (0, 0)
    cur = pl.BlockSpec((tm,), lambda i: (i,), memory_space=pltpu.SMEM)
    nxt = pl.BlockSpec((tm,), lambda i: (jnp.minimum(i + 1, n - 1),), memory_space=pltpu.SMEM)
    return pl.pallas_call(
        _moe_ple_kernel,
        grid=(n,),
        in_specs=[cur, cur, nxt, nxt,
                  pl.BlockSpec(memory_space=pl.ANY),
                  pl.BlockSpec((tm, LANES), row),
                  pl.BlockSpec((tm, d), row),
                  pl.BlockSpec((tm, d_ple), row),
                  pl.BlockSpec((1, d), fixed),
                  pl.BlockSpec((d, d), fixed),
                  pl.BlockSpec((d_ple, d), fixed)],
        out_specs=pl.BlockSpec((tm, d), row),
        out_shape=jax.ShapeDtypeStruct((t, d), F32),
        scratch_shapes=[pltpu.VMEM((2, tm, width), y.dtype), pltpu.VMEM((2, tm, width), y.dtype),
                        pltpu.SemaphoreType.DMA((2,))],
        compiler_params=_params(("arbitrary",)),
        name="moe_ple",
    )(pos1, pos2, pos1, pos2, y, route, h, p2, gain, w_gate, w_up)


def _tile(n, want):
    if n <= want:
        return n
    for cand in range(want, 0, -SUBLANES):
        if n % cand == 0:
            return cand
    return n


def kernel(x, p, mix_norm, w_in, b_forget, q_norm, k_norm, conv_w, conv_b, w_rec_gate, b_rec_gate,
           w_in_gate, b_in_gate, lru_lambda, w_out, ffn_norm, w_router_group, w_router_expert,
           w_expert_gate, w_expert_up, w_expert_down, ple_norm, w_ple_gate, w_ple_up):
    b, s, d = x.shape
    depth = w_in.shape[0]
    n_heads = b_forget.shape[1]
    d_att = n_heads * HEAD_DIM
    d_rnn = conv_w.shape[2]
    n_experts = w_router_expert.shape[2]
    t = b * s
    col_f = 3 * d_att
    col_rx = col_f + n_heads
    assert d_att == d_rnn, "the fused projection output is indexed in equal-width column groups"

    tm_in = _tile(t, 1024)
    tm = _tile(t, 512)
    tq = _tile(s, 512)
    tc_lru = _tile(s, 256)
    tm_disp = _tile(t, 1024)
    tm_comb = _tile(t, 256)
    tile_rows = _tile(t, 512)
    n_rows = TOP_K * t + n_experts * tile_rows
    n_tiles = n_rows // tile_rows

    h = x.reshape(t, d)
    for li in range(depth):
        w = w_in[li]
        w_main = jnp.concatenate([w[:, :col_f], w[:, col_rx:]], axis=1).astype(BF16)
        w_f = jnp.pad(w[:, col_f:col_rx], ((0, 0), (0, LANES - n_heads))).astype(BF16)
        scale = HEAD_DIM ** -0.5
        q_gain = (q_norm[li] * scale).reshape(1, HEAD_DIM)
        k_gain = k_norm[li].reshape(1, HEAD_DIM)
        proj, f_logits = _in_proj(h, mix_norm[li].reshape(1, d), w_main, w_f, q_gain, k_gain, tm=tm_in)
        proj = proj.reshape(b, s, -1)

        b_pad = jnp.pad(b_forget[li], (0, LANES - n_heads)).reshape(1, LANES)
        ct = _forget_cum(f_logits.reshape(b, s, LANES), b_pad, tc=tq)
        c_first = ct[:, :, :, 0].transpose(0, 2, 1).reshape(-1)
        c_last = ct[:, :, :, tq - 1].transpose(0, 2, 1).reshape(-1)
        qk_bound = (2.0 * 1.02 * HEAD_DIM * scale * jnp.max(jnp.abs(q_norm[li]))
                    * jnp.max(jnp.abs(k_norm[li]))).reshape(1).astype(F32)
        att = _fox_attn(proj, ct, c_first, c_last, qk_bound, n_heads=n_heads, tq=tq)

        rnn = _rglru(proj, conv_w[li], conv_b[li].reshape(1, d_rnn), w_rec_gate[li].astype(BF16),
                     b_rec_gate[li].reshape(1, d_rnn), w_in_gate[li].astype(BF16),
                     b_in_gate[li].reshape(1, d_rnn), lru_lambda[li].reshape(1, d_rnn), tc=tc_lru)

        w_router = jnp.pad(jnp.concatenate([w_router_expert[li], w_router_group[li]], axis=1),
                           ((0, 0), (0, LANES - n_experts - N_GROUPS)))
        w_router_hi = w_router.astype(BF16)
        w_router_lo = (w_router - w_router_hi.astype(F32)).astype(BF16)
        h1, hn, route, counts = _out_proj(h, att.reshape(t, d_att), rnn.reshape(t, d_rnn),
                                          w_out[li].astype(BF16), ffn_norm[li].reshape(1, d),
                                          w_router_hi, w_router_lo, tm=tm, n_experts=n_experts)

        cnt = counts[0, :n_experts].astype(jnp.int32)
        padded = (cnt + tile_rows - 1) // tile_rows * tile_rows
        ends = jnp.cumsum(padded)
        offsets = ends - padded
        n_used = (ends[-1] // tile_rows).reshape(1)
        tile_start = jnp.minimum(jnp.arange(n_tiles, dtype=jnp.int32), n_used[0] - 1) * tile_rows
        tile_expert = jnp.sum(ends[None, :] <= tile_start[:, None], axis=1).astype(jnp.int32)
        seg_last = jnp.where(padded > 0, ends // tile_rows - 1, -1)
        tail = n_used[0] + jnp.arange(n_tiles - TOP_K * t // tile_rows, dtype=jnp.int32)
        ztile = jnp.concatenate([seg_last, jnp.where(tail < n_tiles, tail, -1)]).astype(jnp.int32)

        def slot_row(expert_lane, rank_lane):
            e = route[:, expert_lane].astype(jnp.int32)
            hot = e[:, None] == jnp.arange(n_experts, dtype=jnp.int32)[None, :]
            return route[:, rank_lane].astype(jnp.int32) + jnp.sum(jnp.where(hot, offsets[None, :], 0), axis=1)

        pos1 = slot_row(R_E1, R_RANK1)
        pos2 = slot_row(R_E2, R_RANK2)

        xs = _moe_dispatch(pos1, pos2, ztile, hn, n_rows=n_rows, tile_rows=tile_rows, tm=tm_disp)
        y = _moe_experts(tile_expert, n_used, xs, w_expert_gate[li], w_expert_up[li], w_expert_down[li],
                         tile_rows=tile_rows)
        h = _moe_ple(pos1, pos2, y, route, h1, p[li].reshape(t, -1),
                     ple_norm[li].reshape(1, d), w_ple_gate[li].astype(BF16),
                     w_ple_up[li].astype(BF16), tm=tm_comb)
    return h.reshape(b, s, d)
```

```python
import functools

import jax
import jax.numpy as jnp
from jax import lax
from jax.experimental import pallas as pl
from jax.experimental.pallas import tpu as pltpu

HEAD_DIM = 128
RNN_BLOCK = 128
CONV_WIDTH = 4
LRU_C = 8.0
N_GROUPS = 4
EXPERTS_PER_GROUP = 8
TOP_K = 2
EPS = 1e-6
LANES = 128
SUBLANES = 8
VMEM_LIMIT = 56 * 1024 * 1024
ZERO_PROB_LOG = 104.0
R_E1, R_E2, R_W1, R_W2, R_RANK1, R_RANK2 = range(6)
DMA_UNROLL = 8
PLE_CHUNKS = 4

F32 = jnp.float32
BF16 = jnp.bfloat16


def _params(sem):
    return pltpu.CompilerParams(dimension_semantics=sem, vmem_limit_bytes=VMEM_LIMIT)


def _rms(x, gain):
    return x * lax.rsqrt(jnp.mean(x * x, axis=-1, keepdims=True) + EPS) * gain


def _lane_pick(rec, lane_idx):
    lane = lax.broadcasted_iota(jnp.int32, rec.shape, 1)
    return jnp.sum(jnp.where(lane == lane_idx, rec, 0.0), axis=-1, keepdims=True)


def _pack_rows(v):
    half = v.shape[1] // 2
    lo = lax.bitcast_convert_type(v[:, :half].astype(BF16).astype(F32), jnp.uint32)
    hi = lax.bitcast_convert_type(v[:, half:].astype(BF16).astype(F32), jnp.uint32)
    return (lo >> 16) | (hi & jnp.uint32(0xFFFF0000))


def _unpack_rows(w):
    return (lax.bitcast_convert_type(w << 16, F32),
            lax.bitcast_convert_type(w & jnp.uint32(0xFFFF0000), F32))


def _in_proj_kernel(x_ref, g_ref, w_ref, wf_ref, qg_ref, kg_ref, o_ref, f_ref, xn_ref, *, n_heads):
    j = pl.program_id(1)

    @pl.when(j == 0)
    def _():
        xn = _rms(x_ref[...], g_ref[...]).astype(BF16)
        xn_ref[...] = xn
        f_ref[...] = jnp.dot(xn, wf_ref[...], preferred_element_type=F32)

    acc = jnp.dot(xn_ref[...], w_ref[...], preferred_element_type=F32)

    def head_norm(gain_ref):
        for hd in range(n_heads):
            sl = slice(hd * HEAD_DIM, (hd + 1) * HEAD_DIM)
            o_ref[:, sl] = _rms(acc[:, sl], gain_ref[...]).astype(o_ref.dtype)

    @pl.when(j == 0)
    def _():
        head_norm(qg_ref)

    @pl.when(j == 1)
    def _():
        head_norm(kg_ref)

    @pl.when(j >= 2)
    def _():
        o_ref[...] = acc.astype(o_ref.dtype)


def _in_proj(x2, gain, w_main, w_f, q_gain, k_gain, *, tm):
    t, d = x2.shape
    n_col = 5
    d_att = w_main.shape[1] // n_col
    n_heads = d_att // HEAD_DIM
    row = lambda i, j: (i, 0)
    fixed = lambda i, j: (0, 0)
    return pl.pallas_call(
        functools.partial(_in_proj_kernel, n_heads=n_heads),
        grid=(t // tm, n_col),
        in_specs=[
            pl.BlockSpec((tm, d), row),
            pl.BlockSpec((1, d), fixed),
            pl.BlockSpec((d, d_att), lambda i, j: (0, j)),
            pl.BlockSpec((d, LANES), fixed),
            pl.BlockSpec((1, HEAD_DIM), fixed),
            pl.BlockSpec((1, HEAD_DIM), fixed),
        ],
        out_specs=[pl.BlockSpec((tm, d_att), lambda i, j: (i, j)), pl.BlockSpec((tm, LANES), row)],
        out_shape=[jax.ShapeDtypeStruct((t, n_col * d_att), BF16),
                   jax.ShapeDtypeStruct((t, LANES), F32)],
        scratch_shapes=[pltpu.VMEM((tm, d), BF16)],
        compiler_params=_params(("arbitrary", "arbitrary")),
        name="in_proj",
    )(x2, gain, w_main, w_f, q_gain, k_gain)


def _forget_cum_kernel(f_ref, b_ref, ct_ref, carry_ref):
    @pl.when(pl.program_id(1) == 0)
    def _():
        carry_ref[...] = jnp.zeros_like(carry_ref)

    log_f = jax.nn.log_sigmoid(f_ref[0] + b_ref[...])
    tc = log_f.shape[0]
    tri = (lax.broadcasted_iota(jnp.int32, (tc, tc), 0)
           >= lax.broadcasted_iota(jnp.int32, (tc, tc), 1)).astype(F32)
    c = jnp.dot(tri, log_f, preferred_element_type=F32,
                precision=lax.Precision.HIGHEST) + carry_ref[...]
    ct_ref[0, 0] = c.T[:SUBLANES, :]
    carry_ref[...] = c[tc - 1:tc, :]


def _forget_cum(f_logits, b_pad, *, tc):
    b, s, _ = f_logits.shape
    return pl.pallas_call(
        _forget_cum_kernel,
        grid=(b, s // tc),
        in_specs=[pl.BlockSpec((1, tc, LANES), lambda bi, i: (bi, i, 0)),
                  pl.BlockSpec((1, LANES), lambda bi, i: (0, 0))],
        out_specs=pl.BlockSpec((1, 1, SUBLANES, tc), lambda bi, i: (bi, i, 0, 0)),
        out_shape=jax.ShapeDtypeStruct((b, s // tc, SUBLANES, tc), F32),
        scratch_shapes=[pltpu.VMEM((1, LANES), F32)],
        compiler_params=_params(("arbitrary", "arbitrary")),
        name="forget_cum",
    )(f_logits, b_pad)


def _attn_kernel(cfirst_ref, clast_ref, qkb_ref, q_ref, k_ref, v_ref, ct_ref, o_ref,
                 vaug_ref, m_ref, acc_ref, sa_ref, sb_ref):
    bi, h, i = pl.program_id(0), pl.program_id(1), pl.program_id(2)
    tq = q_ref.shape[1]
    base = (bi * pl.num_programs(1) + h) * pl.num_programs(2)

    @pl.when(i == 0)
    def _():
        vaug_ref[:, :HEAD_DIM] = v_ref[0]
        vaug_ref[:, HEAD_DIM:] = jnp.ones((vaug_ref.shape[0], HEAD_DIM), vaug_ref.dtype)

    c_ref0 = cfirst_ref[base + i]
    keep_below = c_ref0 + qkb_ref[0] + ZERO_PROB_LOG
    j_lo = lax.while_loop(
        lambda j: (j > 0) & (clast_ref[base + jnp.maximum(j - 1, 0)] < keep_below),
        lambda j: j - 1, i)

    m_ref[...] = jnp.full_like(m_ref, -jnp.inf)
    acc_ref[...] = jnp.zeros_like(acc_ref)

    def scores(j, s_ref):
        off = pl.multiple_of(j * tq, tq)
        s = lax.dot_general(q_ref[0], k_ref[0, pl.ds(off, tq), :], (((1,), (1,)), ((), ())),
                            preferred_element_type=F32)
        s_ref[...] = s + (c_ref0 - ct_ref[0, j, pl.ds(h, 1), :])

    def accumulate(j, s_ref, masked):
        off = pl.multiple_of(j * tq, tq)
        s = s_ref[...]
        if masked:
            rows = lax.broadcasted_iota(jnp.int32, s.shape, 0)
            cols = lax.broadcasted_iota(jnp.int32, s.shape, 1)
            s = jnp.where(rows >= cols, s, -jnp.inf)
        m_prev = m_ref[...]
        m_new = jnp.maximum(m_prev, jnp.max(s, axis=-1, keepdims=True))
        alpha = jnp.exp(m_prev - m_new)
        pr = jnp.exp(s - m_new).astype(BF16)
        acc_ref[...] = alpha * acc_ref[...] + jnp.dot(
            pr, vaug_ref[pl.ds(off, tq), :], preferred_element_type=F32)
        m_ref[...] = m_new

    n_off = i - j_lo
    odd = n_off % 2
    scores(j_lo, sa_ref)

    @pl.when(odd == 1)
    def _():
        accumulate(j_lo, sa_ref, False)
        scores(j_lo + 1, sa_ref)

    j_even = j_lo + odd

    def pair(kk, carry):
        j = j_even + 2 * kk
        scores(j + 1, sb_ref)
        accumulate(j, sa_ref, False)
        scores(j + 2, sa_ref)
        accumulate(j + 1, sb_ref, False)
        return carry

    lax.fori_loop(0, n_off // 2, pair, 0)
    accumulate(i, sa_ref, True)
    o_ref[0] = (acc_ref[:, :HEAD_DIM] / acc_ref[:, HEAD_DIM:]).astype(o_ref.dtype)


def _fox_attn(proj, ct, c_first, c_last, qk_bound, *, n_heads, tq):
    b, s, _ = proj.shape
    d_att = n_heads * HEAD_DIM
    nq = s // tq
    q_map = lambda bi, h, i, *_: (bi, i, h)
    grid_spec = pltpu.PrefetchScalarGridSpec(
        num_scalar_prefetch=3,
        grid=(b, n_heads, nq),
        in_specs=[
            pl.BlockSpec((1, tq, HEAD_DIM), q_map),
            pl.BlockSpec((1, s, HEAD_DIM), lambda bi, h, i, *_: (bi, 0, n_heads + h)),
            pl.BlockSpec((1, s, HEAD_DIM), lambda bi, h, i, *_: (bi, 0, 2 * n_heads + h)),
            pl.BlockSpec((1, nq, SUBLANES, tq), lambda bi, h, i, *_: (bi, 0, 0, 0)),
        ],
        out_specs=pl.BlockSpec((1, tq, HEAD_DIM), q_map),
        scratch_shapes=[pltpu.VMEM((s, 2 * HEAD_DIM), BF16), pltpu.VMEM((tq, 1), F32),
                        pltpu.VMEM((tq, 2 * HEAD_DIM), F32), pltpu.VMEM((tq, tq), F32),
                        pltpu.VMEM((tq, tq), F32)],
    )
    return pl.pallas_call(
        _attn_kernel,
        grid_spec=grid_spec,
        out_shape=jax.ShapeDtypeStruct((b, s, d_att), BF16),
        compiler_params=_params(("arbitrary", "arbitrary", "arbitrary")),
        name="fox_attn",
    )(c_first, c_last, qk_bound, proj, proj, proj, ct)


def _rglru_kernel(rx_ref, rg_ref, cw_ref, cb_ref, wa_ref, ba_ref, wi_ref, bi_ref, lam_ref,
                  o_ref, xbuf_ref, a_ref, u_ref, h_ref, *, n_blocks):
    nb, tc = rx_ref.shape[0], rx_ref.shape[1]

    @pl.when(pl.program_id(0) == 0)
    def _():
        xbuf_ref[:, 0:SUBLANES, :] = jnp.zeros((nb, SUBLANES, xbuf_ref.shape[2]), F32)
        h_ref[...] = jnp.zeros_like(h_ref)

    neg_sp = -LRU_C * jax.nn.softplus(-lam_ref[...])
    for bb in range(nb):
        xbuf_ref[bb, SUBLANES:SUBLANES + tc, :] = rx_ref[bb].astype(F32)
        y = cb_ref[...]
        for kk in range(CONV_WIDTH):
            off = SUBLANES - (CONV_WIDTH - 1) + kk
            y = y + cw_ref[kk:kk + 1, :] * xbuf_ref[bb, off:off + tc, :]
        xbuf_ref[bb, 0:SUBLANES, :] = xbuf_ref[bb, tc:tc + SUBLANES, :]

        yb = y.astype(BF16)
        for n in range(n_blocks):
            sl = slice(n * RNN_BLOCK, (n + 1) * RNN_BLOCK)
            r = jax.nn.sigmoid(jnp.dot(yb[:, sl], wa_ref[n], preferred_element_type=F32) + ba_ref[:, sl])
            g = jax.nn.sigmoid(jnp.dot(yb[:, sl], wi_ref[n], preferred_element_type=F32) + bi_ref[:, sl])
            log_a = r * neg_sp[:, sl]
            a = jnp.exp(log_a)
            a_ref[bb, :, sl] = a
            u_ref[bb, :, sl] = jnp.sqrt(1.0 - a * a) * (g * y[:, sl])

    def scan_body(t, hs):
        out = []
        for bb in range(nb):
            hnew = a_ref[bb, pl.ds(t, 1), :] * hs[bb] + u_ref[bb, pl.ds(t, 1), :]
            u_ref[bb, pl.ds(t, 1), :] = hnew
            out.append(hnew)
        return tuple(out)

    hs = lax.fori_loop(0, tc, scan_body, tuple(h_ref[bb] for bb in range(nb)), unroll=8)
    for bb in range(nb):
        h_ref[bb] = hs[bb]
        o_ref[bb] = (jax.nn.gelu(rg_ref[bb].astype(F32), approximate=True) * u_ref[bb]).astype(o_ref.dtype)


def _rglru(proj, conv_w, conv_b, w_a, b_a, w_i, b_i, lam, *, tc):
    b, s, _ = proj.shape
    c = conv_w.shape[1]
    n_blocks = c // RNN_BLOCK
    fixed2 = lambda i: (0, 0)
    fixed3 = lambda i: (0, 0, 0)
    return pl.pallas_call(
        functools.partial(_rglru_kernel, n_blocks=n_blocks),
        grid=(s // tc,),
        in_specs=[
            pl.BlockSpec((b, tc, c), lambda i: (0, i, 3)),
            pl.BlockSpec((b, tc, c), lambda i: (0, i, 4)),
            pl.BlockSpec((CONV_WIDTH, c), fixed2),
            pl.BlockSpec((1, c), fixed2),
            pl.BlockSpec((n_blocks, RNN_BLOCK, RNN_BLOCK), fixed3),
            pl.BlockSpec((1, c), fixed2),
            pl.BlockSpec((n_blocks, RNN_BLOCK, RNN_BLOCK), fixed3),
            pl.BlockSpec((1, c), fixed2),
            pl.BlockSpec((1, c), fixed2),
        ],
        out_specs=pl.BlockSpec((b, tc, c), lambda i: (0, i, 0)),
        out_shape=jax.ShapeDtypeStruct((b, s, c), BF16),
        scratch_shapes=[pltpu.VMEM((b, tc + SUBLANES, c), F32), pltpu.VMEM((b, tc, c), F32),
                        pltpu.VMEM((b, tc, c), F32), pltpu.VMEM((b, 1, c), F32)],
        compiler_params=_params(("arbitrary",)),
        name="rglru",
    )(proj, proj, conv_w, conv_b, w_a, b_a, w_i, b_i, lam)


def _out_proj_kernel(x_ref, att_ref, rnn_ref, wo_ref, g_ref, wrh_ref, wrl_ref,
                     h_ref, hn_ref, route_ref, counts_ref, carry_ref, *, d_att, n_experts):
    @pl.when(pl.program_id(0) == 0)
    def _():
        carry_ref[...] = jnp.zeros_like(carry_ref)

    h = (x_ref[...]
         + jnp.dot(att_ref[...], wo_ref[0:d_att, :], preferred_element_type=F32)
         + jnp.dot(rnn_ref[...], wo_ref[d_att:, :], preferred_element_type=F32))
    h_ref[...] = h
    hn = _rms(h, g_ref[...])
    hn_ref[...] = _pack_rows(hn)

    hn_hi = hn.astype(BF16)
    hn_lo = (hn - hn_hi.astype(F32)).astype(BF16)
    logits = (jnp.dot(hn_hi, wrh_ref[...], preferred_element_type=F32)
              + (jnp.dot(hn_hi, wrl_ref[...], preferred_element_type=F32)
                 + jnp.dot(hn_lo, wrh_ref[...], preferred_element_type=F32)))
    tm = logits.shape[0]
    lane = lax.broadcasted_iota(jnp.int32, logits.shape, 1)
    neg = -jnp.inf

    def first_argmax(vals, vmax):
        return jnp.min(jnp.where(vals == vmax, lane, LANES), axis=-1, keepdims=True)

    g_mask = (lane >= n_experts) & (lane < n_experts + N_GROUPS)
    g_log = jnp.where(g_mask, logits, neg)
    g_max = jnp.max(g_log, axis=-1, keepdims=True)
    g_w = 1.0 / jnp.sum(jnp.exp(g_log - g_max), axis=-1, keepdims=True)
    g_idx = first_argmax(g_log, g_max) - n_experts

    e_lo = g_idx * EXPERTS_PER_GROUP
    e_log = jnp.where((lane >= e_lo) & (lane < e_lo + EXPERTS_PER_GROUP), logits, neg)
    l1 = jnp.max(e_log, axis=-1, keepdims=True)
    i1 = first_argmax(e_log, l1)
    e_log2 = jnp.where(lane == i1, neg, e_log)
    l2 = jnp.max(e_log2, axis=-1, keepdims=True)
    i2 = first_argmax(e_log2, l2)
    e2 = jnp.exp(l2 - l1)
    w1 = g_w / (1.0 + e2)
    w2 = g_w * e2 / (1.0 + e2)

    hot1 = lane == i1
    hot2 = lane == i2
    hot = jnp.where(hot1 | hot2, 1.0, 0.0)
    before = (lax.broadcasted_iota(jnp.int32, (tm, tm), 0)
              > lax.broadcasted_iota(jnp.int32, (tm, tm), 1)).astype(BF16)
    seen = jnp.dot(before, hot.astype(BF16), preferred_element_type=F32) + carry_ref[...]
    rank1 = jnp.sum(jnp.where(hot1, seen, 0.0), axis=-1, keepdims=True)
    rank2 = jnp.sum(jnp.where(hot2, seen, 0.0), axis=-1, keepdims=True)
    carry_ref[...] += jnp.sum(hot, axis=0, keepdims=True)
    counts_ref[...] = jnp.broadcast_to(carry_ref[...], counts_ref.shape)

    rec = jnp.zeros(logits.shape, F32)
    for lane_idx, col in ((R_E1, i1.astype(F32)), (R_E2, i2.astype(F32)), (R_W1, w1), (R_W2, w2),
                          (R_RANK1, rank1), (R_RANK2, rank2)):
        rec = jnp.where(lane == lane_idx, col, rec)
    route_ref[...] = rec


def _out_proj(x2, att, rnn, w_out, gain, w_router_hi, w_router_lo, *, tm, n_experts):
    t, d = x2.shape
    d_att = att.shape[1]
    d_rnn = rnn.shape[1]
    row = lambda i: (i, 0)
    fixed = lambda i: (0, 0)
    return pl.pallas_call(
        functools.partial(_out_proj_kernel, d_att=d_att, n_experts=n_experts),
        grid=(t // tm,),
        in_specs=[
            pl.BlockSpec((tm, d), row),
            pl.BlockSpec((tm, d_att), row),
            pl.BlockSpec((tm, d_rnn), row),
            pl.BlockSpec((d_att + d_rnn, d), fixed),
            pl.BlockSpec((1, d), fixed),
            pl.BlockSpec((d, LANES), fixed),
            pl.BlockSpec((d, LANES), fixed),
        ],
        out_specs=[pl.BlockSpec((tm, d), row), pl.BlockSpec((tm, d // 2), row),
                   pl.BlockSpec((tm, LANES), row), pl.BlockSpec((SUBLANES, LANES), fixed)],
        out_shape=[jax.ShapeDtypeStruct((t, d), F32), jax.ShapeDtypeStruct((t, d // 2), jnp.uint32),
                   jax.ShapeDtypeStruct((t, LANES), F32), jax.ShapeDtypeStruct((SUBLANES, LANES), F32)],
        scratch_shapes=[pltpu.VMEM((1, LANES), F32)],
        compiler_params=_params(("arbitrary",)),
        name="out_proj",
    )(x2, att, rnn, w_out, gain, w_router_hi, w_router_lo)


def _row_copy(src_hbm, src_row, dst_ref, dst_row, sem):
    return pltpu.make_async_copy(src_hbm.at[pl.ds(src_row, 1)], dst_ref.at[pl.ds(dst_row, 1)], sem)


def _dispatch_kernel(p1_ref, p2_ref, ztile_ref, hn_ref, xs_hbm, zero_ref, zsem, sem):
    i = pl.program_id(0)
    tm = hn_ref.shape[0]
    tile_rows = zero_ref.shape[0]

    @pl.when(i == 0)
    def _():
        zero_ref[...] = jnp.zeros_like(zero_ref)

        def zero_copy(z):
            start = pl.multiple_of(ztile_ref[z] * tile_rows, tile_rows)
            return pltpu.make_async_copy(zero_ref, xs_hbm.at[pl.ds(start, tile_rows)], zsem)

        for z in range(ztile_ref.shape[0]):
            @pl.when(ztile_ref[z] >= 0)
            def _():
                zero_copy(z).start()
        for z in range(ztile_ref.shape[0]):
            @pl.when(ztile_ref[z] >= 0)
            def _():
                zero_copy(z).wait()

    def issue(t, carry):
        _row_copy(hn_ref, t, xs_hbm, p1_ref[t], sem).start(priority=0)
        _row_copy(hn_ref, t, xs_hbm, p2_ref[t], sem).start(priority=1)
        return carry

    lax.fori_loop(0, tm, issue, 0, unroll=DMA_UNROLL)

    def drain(t, carry):
        _row_copy(hn_ref, 0, xs_hbm, 0, sem).wait()
        return carry

    lax.fori_loop(0, TOP_K * tm, drain, 0, unroll=DMA_UNROLL)


def _moe_dispatch(pos1, pos2, ztile, hn, *, n_rows, tile_rows, tm):
    t, width = hn.shape
    smem_blk = pl.BlockSpec((tm,), lambda i: (i,), memory_space=pltpu.SMEM)
    return pl.pallas_call(
        _dispatch_kernel,
        grid=(t // tm,),
        in_specs=[smem_blk, smem_blk, pl.BlockSpec(memory_space=pltpu.SMEM),
                  pl.BlockSpec((tm, width), lambda i: (i, 0))],
        out_specs=pl.BlockSpec(memory_space=pl.ANY),
        out_shape=jax.ShapeDtypeStruct((n_rows, width), hn.dtype),
        scratch_shapes=[pltpu.VMEM((tile_rows, width), hn.dtype), pltpu.SemaphoreType.DMA,
                        pltpu.SemaphoreType.DMA],
        compiler_params=_params(("arbitrary",)),
        name="moe_dispatch",
    )(pos1, pos2, ztile, hn)


def _experts_kernel(texp_ref, nused_ref, x_ref, wg_ref, wu_ref, wd_ref, y_ref, wgu_bf_ref, wd_bf_ref):
    i = pl.program_id(0)
    d_e = wd_ref.shape[1]

    @pl.when((i == 0) | (texp_ref[i] != texp_ref[jnp.maximum(i - 1, 0)]))
    def _():
        wgu_bf_ref[:, :d_e] = wg_ref[0].astype(BF16)
        wgu_bf_ref[:, d_e:] = wu_ref[0].astype(BF16)
        wd_bf_ref[...] = wd_ref[0].astype(BF16)

    @pl.when(i < nused_ref[0])
    def _():
        xa, xb = _unpack_rows(x_ref[...])
        x = jnp.concatenate([xa.astype(BF16), xb.astype(BF16)], axis=1)
        gu = jnp.dot(x, wgu_bf_ref[...], preferred_element_type=F32)
        hid = jax.nn.silu(gu[:, :d_e]) * gu[:, d_e:]
        y_ref[...] = _pack_rows(jnp.dot(hid.astype(BF16), wd_bf_ref[...], preferred_element_type=F32))

    @pl.when(i >= nused_ref[0])
    def _():
        y_ref[...] = jnp.zeros_like(y_ref)


def _moe_experts(tile_expert, n_used, xs, w_gate, w_up, w_down, *, tile_rows):
    n_rows, width = xs.shape
    n_tiles = n_rows // tile_rows
    _, d, d_e = w_gate.shape
    row = lambda i, te, nu: (jnp.minimum(i, nu[0] - 1), 0)
    wmap = lambda i, te, nu: (te[i], 0, 0)
    grid_spec = pltpu.PrefetchScalarGridSpec(
        num_scalar_prefetch=2,
        grid=(n_tiles,),
        in_specs=[pl.BlockSpec((tile_rows, width), row),
                  pl.BlockSpec((1, d, d_e), wmap),
                  pl.BlockSpec((1, d, d_e), wmap),
                  pl.BlockSpec((1, d_e, d), wmap)],
        out_specs=pl.BlockSpec((tile_rows, width), lambda i, te, nu: (i, 0)),
        scratch_shapes=[pltpu.VMEM((d, 2 * d_e), BF16), pltpu.VMEM((d_e, d), BF16)],
    )
    return pl.pallas_call(
        _experts_kernel,
        grid_spec=grid_spec,
        out_shape=jax.ShapeDtypeStruct((n_rows, width), xs.dtype),
        compiler_params=_params(("arbitrary",)),
        name="moe_experts",
    )(tile_expert, n_used, xs, w_gate, w_up, w_down)


def _moe_ple_kernel(p1_ref, p2_ref, p1n_ref, p2n_ref,
                    y_hbm, route_ref, h_ref, p_ref, g_ref, wg_ref, wu_ref, o_ref,
                    buf1_ref, buf2_ref, sem):
    i = pl.program_id(0)
    n = pl.num_programs(0)
    tm, d = h_ref.shape
    slot = i % 2
    nxt = 1 - slot

    def start_rows(p1s, p2s, t, slot_idx):
        _row_copy(y_hbm, p1s[t], buf1_ref.at[slot_idx], t, sem.at[slot_idx]).start(priority=0)
        _row_copy(y_hbm, p2s[t], buf2_ref.at[slot_idx], t, sem.at[slot_idx]).start(priority=1)

    def drain(slot_idx):
        def body(t, carry):
            _row_copy(y_hbm, 0, buf1_ref.at[slot_idx], 0, sem.at[slot_idx]).wait()
            return carry
        lax.fori_loop(0, TOP_K * tm, body, 0, unroll=DMA_UNROLL)

    @pl.when(i == 0)
    def _():
        def first(t, carry):
            start_rows(p1_ref, p2_ref, t, 0)
            return carry
        lax.fori_loop(0, tm, first, 0, unroll=DMA_UNROLL)

    drain(slot)

    rec = route_ref[...]
    w1 = _lane_pick(rec, R_W1)
    w2 = _lane_pick(rec, R_W2)
    y1a, y1b = _unpack_rows(buf1_ref[slot])
    y2a, y2b = _unpack_rows(buf2_ref[slot])
    h = h_ref[...] + jnp.concatenate([w1 * y1a + w2 * y2a, w1 * y1b + w2 * y2b], axis=1)
    hn = _rms(h, g_ref[...]).astype(BF16)
    up = jnp.dot(p_ref[...].astype(BF16), wu_ref[...], preferred_element_type=F32)

    cw = d // PLE_CHUNKS
    rows = tm // PLE_CHUNKS
    for c in range(PLE_CHUNKS):
        sl = slice(c * cw, (c + 1) * cw)
        gate = jax.nn.sigmoid(jnp.dot(hn, wg_ref[:, sl], preferred_element_type=F32))
        o_ref[:, sl] = h[:, sl] + gate * up[:, sl]
        for t in range(c * rows, (c + 1) * rows):
            start_rows(p1n_ref, p2n_ref, t, nxt)

    @pl.when(i == n - 1)
    def _():
        drain(nxt)


def _moe_ple(pos1, pos2, y, route, h, p2, gain, w_gate, w_up, *, tm):
    t, d = h.shape
    d_ple = p2.shape[1]
    width = y.shape[1]
    n = t // tm
    row = lambda i: (i, 0)
    fixed = lambda i: (0, 0)
    cur = pl.BlockSpec((tm,), lambda i: (i,), memory_space=pltpu.SMEM)
    nxt = pl.BlockSpec((tm,), lambda i: (jnp.minimum(i + 1, n - 1),), memory_space=pltpu.SMEM)
    return pl.pallas_call(
        _moe_ple_kernel,
        grid=(n,),
        in_specs=[cur, cur, nxt, nxt,
                  pl.BlockSpec(memory_space=pl.ANY),
                  pl.BlockSpec((tm, LANES), row),
                  pl.BlockSpec((tm, d), row),
                  pl.BlockSpec((tm, d_ple), row),
                  pl.BlockSpec((1, d), fixed),
                  pl.BlockSpec((d, d), fixed),
                  pl.BlockSpec((d_ple, d), fixed)],
        out_specs=pl.BlockSpec((tm, d), row),
        out_shape=jax.ShapeDtypeStruct((t, d), F32),
        scratch_shapes=[pltpu.VMEM((2, tm, width), y.dtype), pltpu.VMEM((2, tm, width), y.dtype),
                        pltpu.SemaphoreType.DMA((2,))],
        compiler_params=_params(("arbitrary",)),
        name="moe_ple",
    )(pos1, pos2, pos1, pos2, y, route, h, p2, gain, w_gate, w_up)


def _tile(n, want):
    if n <= want:
        return n
    for cand in range(want, 0, -SUBLANES):
        if n % cand == 0:
            return cand
    return n


def kernel(x, p, mix_norm, w_in, b_forget, q_norm, k_norm, conv_w, conv_b, w_rec_gate, b_rec_gate,
           w_in_gate, b_in_gate, lru_lambda, w_out, ffn_norm, w_router_group, w_router_expert,
           w_expert_gate, w_expert_up, w_expert_down, ple_norm, w_ple_gate, w_ple_up):
    b, s, d = x.shape
    depth = w_in.shape[0]
    n_heads = b_forget.shape[1]
    d_att = n_heads * HEAD_DIM
    d_rnn = conv_w.shape[2]
    n_experts = w_router_expert.shape[2]
    t = b * s
    col_f = 3 * d_att
    col_rx = col_f + n_heads
    assert d_att == d_rnn, "the fused projection output is indexed in equal-width column groups"

    tm_in = _tile(t, 1024)
    tm = _tile(t, 512)
    tq = _tile(s, 512)
    tc_lru = _tile(s, 256)
    tm_disp = _tile(t, 1024)
    tm_comb = _tile(t, 256)
    tile_rows = _tile(t, 512)
    n_rows = TOP_K * t + n_experts * tile_rows
    n_tiles = n_rows // tile_rows

    h = x.reshape(t, d)
    for li in range(depth):
        w = w_in[li]
        w_main = jnp.concatenate([w[:, :col_f], w[:, col_rx:]], axis=1).astype(BF16)
        w_f = jnp.pad(w[:, col_f:col_rx], ((0, 0), (0, LANES - n_heads))).astype(BF16)
        scale = HEAD_DIM ** -0.5
        q_gain = (q_norm[li] * scale).reshape(1, HEAD_DIM)
        k_gain = k_norm[li].reshape(1, HEAD_DIM)
        proj, f_logits = _in_proj(h, mix_norm[li].reshape(1, d), w_main, w_f, q_gain, k_gain, tm=tm_in)
        proj = proj.reshape(b, s, -1)

        b_pad = jnp.pad(b_forget[li], (0, LANES - n_heads)).reshape(1, LANES)
        ct = _forget_cum(f_logits.reshape(b, s, LANES), b_pad, tc=tq)
        c_first = ct[:, :, :, 0].transpose(0, 2, 1).reshape(-1)
        c_last = ct[:, :, :, tq - 1].transpose(0, 2, 1).reshape(-1)
        qk_bound = (2.0 * 1.02 * HEAD_DIM * scale * jnp.max(jnp.abs(q_norm[li]))
                    * jnp.max(jnp.abs(k_norm[li]))).reshape(1).astype(F32)
        att = _fox_attn(proj, ct, c_first, c_last, qk_bound, n_heads=n_heads, tq=tq)

        rnn = _rglru(proj, conv_w[li], conv_b[li].reshape(1, d_rnn), w_rec_gate[li].astype(BF16),
                     b_rec_gate[li].reshape(1, d_rnn), w_in_gate[li].astype(BF16),
                     b_in_gate[li].reshape(1, d_rnn), lru_lambda[li].reshape(1, d_rnn), tc=tc_lru)

        w_router = jnp.pad(jnp.concatenate([w_router_expert[li], w_router_group[li]], axis=1),
                           ((0, 0), (0, LANES - n_experts - N_GROUPS)))
        w_router_hi = w_router.astype(BF16)
        w_router_lo = (w_router - w_router_hi.astype(F32)).astype(BF16)
        h1, hn, route, counts = _out_proj(h, att.reshape(t, d_att), rnn.reshape(t, d_rnn),
                                          w_out[li].astype(BF16), ffn_norm[li].reshape(1, d),
                                          w_router_hi, w_router_lo, tm=tm, n_experts=n_experts)

        cnt = counts[0, :n_experts].astype(jnp.int32)
        padded = (cnt + tile_rows - 1) // tile_rows * tile_rows
        ends = jnp.cumsum(padded)
        offsets = ends - padded
        n_used = (ends[-1] // tile_rows).reshape(1)
        tile_start = jnp.minimum(jnp.arange(n_tiles, dtype=jnp.int32), n_used[0] - 1) * tile_rows
        tile_expert = jnp.sum(ends[None, :] <= tile_start[:, None], axis=1).astype(jnp.int32)
        seg_last = jnp.where(padded > 0, ends // tile_rows - 1, -1)
        tail = n_used[0] + jnp.arange(n_tiles - TOP_K * t // tile_rows, dtype=jnp.int32)
        ztile = jnp.concatenate([seg_last, jnp.where(tail < n_tiles, tail, -1)]).astype(jnp.int32)

        def slot_row(expert_lane, rank_lane):
            e = route[:, expert_lane].astype(jnp.int32)
            hot = e[:, None] == jnp.arange(n_experts, dtype=jnp.int32)[None, :]
            return route[:, rank_lane].astype(jnp.int32) + jnp.sum(jnp.where(hot, offsets[None, :], 0), axis=1)

        pos1 = slot_row(R_E1, R_RANK1)
        pos2 = slot_row(R_E2, R_RANK2)

        xs = _moe_dispatch(pos1, pos2, ztile, hn, n_rows=n_rows, tile_rows=tile_rows, tm=tm_disp)
        y = _moe_experts(tile_expert, n_used, xs, w_expert_gate[li], w_expert_up[li], w_expert_down[li],
                         tile_rows=tile_rows)
        h = _moe_ple(pos1, pos2, y, route, h1, p[li].reshape(t, -1),
                     ple_norm[li].reshape(1, d), w_ple_gate[li].astype(BF16),
                     w_ple_up[li].astype(BF16), tm=tm_comb)
    return h.reshape(b, s, d)
```

```python
import functools

import jax
import jax.numpy as jnp
from jax import lax
from jax.experimental import pallas as pl
from jax.experimental.pallas import tpu as pltpu

HEAD_DIM = 128
RNN_BLOCK = 128
CONV_WIDTH = 4
LRU_C = 8.0
N_GROUPS = 4
EXPERTS_PER_GROUP = 8
TOP_K = 2
EPS = 1e-6
LANES = 128
SUBLANES = 8
VMEM_LIMIT = 56 * 1024 * 1024
ZERO_PROB_LOG = 104.0
R_E1, R_E2, R_W1, R_W2, R_RANK1, R_RANK2 = range(6)
DMA_UNROLL = 8
PLE_CHUNKS = 4

F32 = jnp.float32
BF16 = jnp.bfloat16


def _params(sem):
    return pltpu.CompilerParams(dimension_semantics=sem, vmem_limit_bytes=VMEM_LIMIT)


def _rms(x, gain):
    return x * lax.rsqrt(jnp.mean(x * x, axis=-1, keepdims=True) + EPS) * gain


def _lane_pick(rec, lane_idx):
    lane = lax.broadcasted_iota(jnp.int32, rec.shape, 1)
    return jnp.sum(jnp.where(lane == lane_idx, rec, 0.0), axis=-1, keepdims=True)


def _in_proj_kernel(x_ref, g_ref, w_ref, wf_ref, qg_ref, kg_ref, o_ref, f_ref, xn_ref, *, n_heads):
    j = pl.program_id(1)

    @pl.when(j == 0)
    def _():
        xn = _rms(x_ref[...], g_ref[...]).astype(BF16)
        xn_ref[...] = xn
        f_ref[...] = jnp.dot(xn, wf_ref[...], preferred_element_type=F32)

    acc = jnp.dot(xn_ref[...], w_ref[...], preferred_element_type=F32)

    def head_norm(gain_ref):
        for hd in range(n_heads):
            sl = slice(hd * HEAD_DIM, (hd + 1) * HEAD_DIM)
            o_ref[:, sl] = _rms(acc[:, sl], gain_ref[...]).astype(o_ref.dtype)

    @pl.when(j == 0)
    def _():
        head_norm(qg_ref)

    @pl.when(j == 1)
    def _():
        head_norm(kg_ref)

    @pl.when(j >= 2)
    def _():
        o_ref[...] = acc.astype(o_ref.dtype)


def _in_proj(x2, gain, w_main, w_f, q_gain, k_gain, *, tm):
    t, d = x2.shape
    n_col = 5
    d_att = w_main.shape[1] // n_col
    n_heads = d_att // HEAD_DIM
    row = lambda i, j: (i, 0)
    fixed = lambda i, j: (0, 0)
    return pl.pallas_call(
        functools.partial(_in_proj_kernel, n_heads=n_heads),
        grid=(t // tm, n_col),
        in_specs=[
            pl.BlockSpec((tm, d), row),
            pl.BlockSpec((1, d), fixed),
            pl.BlockSpec((d, d_att), lambda i, j: (0, j)),
            pl.BlockSpec((d, LANES), fixed),
            pl.BlockSpec((1, HEAD_DIM), fixed),
            pl.BlockSpec((1, HEAD_DIM), fixed),
        ],
        out_specs=[pl.BlockSpec((tm, d_att), lambda i, j: (i, j)), pl.BlockSpec((tm, LANES), row)],
        out_shape=[jax.ShapeDtypeStruct((t, n_col * d_att), BF16),
                   jax.ShapeDtypeStruct((t, LANES), F32)],
        scratch_shapes=[pltpu.VMEM((tm, d), BF16)],
        compiler_params=_params(("arbitrary", "arbitrary")),
        name="in_proj",
    )(x2, gain, w_main, w_f, q_gain, k_gain)


def _forget_cum_kernel(f_ref, b_ref, ct_ref, carry_ref):
    @pl.when(pl.program_id(1) == 0)
    def _():
        carry_ref[...] = jnp.zeros_like(carry_ref)

    log_f = jax.nn.log_sigmoid(f_ref[0] + b_ref[...])
    tc = log_f.shape[0]
    tri = (lax.broadcasted_iota(jnp.int32, (tc, tc), 0)
           >= lax.broadcasted_iota(jnp.int32, (tc, tc), 1)).astype(F32)
    c = jnp.dot(tri, log_f, preferred_element_type=F32,
                precision=lax.Precision.HIGHEST) + carry_ref[...]
    ct_ref[0, 0] = c.T[:SUBLANES, :]
    carry_ref[...] = c[tc - 1:tc, :]


def _forget_cum(f_logits, b_pad, *, tc):
    b, s, _ = f_logits.shape
    return pl.pallas_call(
        _forget_cum_kernel,
        grid=(b, s // tc),
        in_specs=[pl.BlockSpec((1, tc, LANES), lambda bi, i: (bi, i, 0)),
                  pl.BlockSpec((1, LANES), lambda bi, i: (0, 0))],
        out_specs=pl.BlockSpec((1, 1, SUBLANES, tc), lambda bi, i: (bi, i, 0, 0)),
        out_shape=jax.ShapeDtypeStruct((b, s // tc, SUBLANES, tc), F32),
        scratch_shapes=[pltpu.VMEM((1, LANES), F32)],
        compiler_params=_params(("arbitrary", "arbitrary")),
        name="forget_cum",
    )(f_logits, b_pad)


def _attn_kernel(cfirst_ref, clast_ref, qkb_ref, q_ref, k_ref, v_ref, ct_ref, o_ref,
                 vaug_ref, m_ref, acc_ref, sa_ref, sb_ref):
    bi, h, i = pl.program_id(0), pl.program_id(1), pl.program_id(2)
    tq = q_ref.shape[1]
    base = (bi * pl.num_programs(1) + h) * pl.num_programs(2)

    @pl.when(i == 0)
    def _():
        vaug_ref[:, :HEAD_DIM] = v_ref[0]
        vaug_ref[:, HEAD_DIM:] = jnp.ones((vaug_ref.shape[0], HEAD_DIM), vaug_ref.dtype)

    c_ref0 = cfirst_ref[base + i]
    keep_below = c_ref0 + qkb_ref[0] + ZERO_PROB_LOG
    j_lo = lax.while_loop(
        lambda j: (j > 0) & (clast_ref[base + jnp.maximum(j - 1, 0)] < keep_below),
        lambda j: j - 1, i)

    m_ref[...] = jnp.full_like(m_ref, -jnp.inf)
    acc_ref[...] = jnp.zeros_like(acc_ref)

    def scores(j, s_ref):
        off = pl.multiple_of(j * tq, tq)
        s = lax.dot_general(q_ref[0], k_ref[0, pl.ds(off, tq), :], (((1,), (1,)), ((), ())),
                            preferred_element_type=F32)
        s_ref[...] = s + (c_ref0 - ct_ref[0, j, pl.ds(h, 1), :])

    def accumulate(j, s_ref, masked):
        off = pl.multiple_of(j * tq, tq)
        s = s_ref[...]
        if masked:
            rows = lax.broadcasted_iota(jnp.int32, s.shape, 0)
            cols = lax.broadcasted_iota(jnp.int32, s.shape, 1)
            s = jnp.where(rows >= cols, s, -jnp.inf)
        m_prev = m_ref[...]
        m_new = jnp.maximum(m_prev, jnp.max(s, axis=-1, keepdims=True))
        alpha = jnp.exp(m_prev - m_new)
        pr = jnp.exp(s - m_new).astype(BF16)
        acc_ref[...] = alpha * acc_ref[...] + jnp.dot(
            pr, vaug_ref[pl.ds(off, tq), :], preferred_element_type=F32)
        m_ref[...] = m_new

    n_off = i - j_lo
    odd = n_off % 2
    scores(j_lo, sa_ref)

    @pl.when(odd == 1)
    def _():
        accumulate(j_lo, sa_ref, False)
        scores(j_lo + 1, sa_ref)

    j_even = j_lo + odd

    def pair(kk, carry):
        j = j_even + 2 * kk
        scores(j + 1, sb_ref)
        accumulate(j, sa_ref, False)
        scores(j + 2, sa_ref)
        accumulate(j + 1, sb_ref, False)
        return carry

    lax.fori_loop(0, n_off // 2, pair, 0)
    accumulate(i, sa_ref, True)
    o_ref[0] = (acc_ref[:, :HEAD_DIM] / acc_ref[:, HEAD_DIM:]).astype(o_ref.dtype)


def _fox_attn(proj, ct, c_first, c_last, qk_bound, *, n_heads, tq):
    b, s, _ = proj.shape
    d_att = n_heads * HEAD_DIM
    nq = s // tq
    q_map = lambda bi, h, i, *_: (bi, i, h)
    grid_spec = pltpu.PrefetchScalarGridSpec(
        num_scalar_prefetch=3,
        grid=(b, n_heads, nq),
        in_specs=[
            pl.BlockSpec((1, tq, HEAD_DIM), q_map),
            pl.BlockSpec((1, s, HEAD_DIM), lambda bi, h, i, *_: (bi, 0, n_heads + h)),
            pl.BlockSpec((1, s, HEAD_DIM), lambda bi, h, i, *_: (bi, 0, 2 * n_heads + h)),
            pl.BlockSpec((1, nq, SUBLANES, tq), lambda bi, h, i, *_: (bi, 0, 0, 0)),
        ],
        out_specs=pl.BlockSpec((1, tq, HEAD_DIM), q_map),
        scratch_shapes=[pltpu.VMEM((s, 2 * HEAD_DIM), BF16), pltpu.VMEM((tq, 1), F32),
                        pltpu.VMEM((tq, 2 * HEAD_DIM), F32), pltpu.VMEM((tq, tq), F32),
                        pltpu.VMEM((tq, tq), F32)],
    )
    return pl.pallas_call(
        _attn_kernel,
        grid_spec=grid_spec,
        out_shape=jax.ShapeDtypeStruct((b, s, d_att), BF16),
        compiler_params=_params(("arbitrary", "arbitrary", "arbitrary")),
        name="fox_attn",
    )(c_first, c_last, qk_bound, proj, proj, proj, ct)


def _rglru_kernel(rx_ref, rg_ref, cw_ref, cb_ref, wa_ref, ba_ref, wi_ref, bi_ref, lam_ref,
                  o_ref, xbuf_ref, a_ref, u_ref, h_ref, *, n_blocks):
    nb, tc = rx_ref.shape[0], rx_ref.shape[1]

    @pl.when(pl.program_id(0) == 0)
    def _():
        xbuf_ref[:, 0:SUBLANES, :] = jnp.zeros((nb, SUBLANES, xbuf_ref.shape[2]), F32)
        h_ref[...] = jnp.zeros_like(h_ref)

    neg_sp = -LRU_C * jax.nn.softplus(-lam_ref[...])
    for bb in range(nb):
        xbuf_ref[bb, SUBLANES:SUBLANES + tc, :] = rx_ref[bb].astype(F32)
        y = cb_ref[...]
        for kk in range(CONV_WIDTH):
            off = SUBLANES - (CONV_WIDTH - 1) + kk
            y = y + cw_ref[kk:kk + 1, :] * xbuf_ref[bb, off:off + tc, :]
        xbuf_ref[bb, 0:SUBLANES, :] = xbuf_ref[bb, tc:tc + SUBLANES, :]

        yb = y.astype(BF16)
        for n in range(n_blocks):
            sl = slice(n * RNN_BLOCK, (n + 1) * RNN_BLOCK)
            r = jax.nn.sigmoid(jnp.dot(yb[:, sl], wa_ref[n], preferred_element_type=F32) + ba_ref[:, sl])
            g = jax.nn.sigmoid(jnp.dot(yb[:, sl], wi_ref[n], preferred_element_type=F32) + bi_ref[:, sl])
            log_a = r * neg_sp[:, sl]
            a = jnp.exp(log_a)
            a_ref[bb, :, sl] = a
            u_ref[bb, :, sl] = jnp.sqrt(1.0 - a * a) * (g * y[:, sl])

    def scan_body(t, hs):
        out = []
        for bb in range(nb):
            hnew = a_ref[bb, pl.ds(t, 1), :] * hs[bb] + u_ref[bb, pl.ds(t, 1), :]
            u_ref[bb, pl.ds(t, 1), :] = hnew
            out.append(hnew)
        return tuple(out)

    hs = lax.fori_loop(0, tc, scan_body, tuple(h_ref[bb] for bb in range(nb)), unroll=8)
    for bb in range(nb):
        h_ref[bb] = hs[bb]
        o_ref[bb] = (jax.nn.gelu(rg_ref[bb].astype(F32), approximate=True) * u_ref[bb]).astype(o_ref.dtype)


def _rglru(proj, conv_w, conv_b, w_a, b_a, w_i, b_i, lam, *, tc):
    b, s, _ = proj.shape
    c = conv_w.shape[1]
    n_blocks = c // RNN_BLOCK
    fixed2 = lambda i: (0, 0)
    fixed3 = lambda i: (0, 0, 0)
    return pl.pallas_call(
        functools.partial(_rglru_kernel, n_blocks=n_blocks),
        grid=(s // tc,),
        in_specs=[
            pl.BlockSpec((b, tc, c), lambda i: (0, i, 3)),
            pl.BlockSpec((b, tc, c), lambda i: (0, i, 4)),
            pl.BlockSpec((CONV_WIDTH, c), fixed2),
            pl.BlockSpec((1, c), fixed2),
            pl.BlockSpec((n_blocks, RNN_BLOCK, RNN_BLOCK), fixed3),
            pl.BlockSpec((1, c), fixed2),
            pl.BlockSpec((n_blocks, RNN_BLOCK, RNN_BLOCK), fixed3),
            pl.BlockSpec((1, c), fixed2),
            pl.BlockSpec((1, c), fixed2),
        ],
        out_specs=pl.BlockSpec((b, tc, c), lambda i: (0, i, 0)),
        out_shape=jax.ShapeDtypeStruct((b, s, c), BF16),
        scratch_shapes=[pltpu.VMEM((b, tc + SUBLANES, c), F32), pltpu.VMEM((b, tc, c), F32),
                        pltpu.VMEM((b, tc, c), F32), pltpu.VMEM((b, 1, c), F32)],
        compiler_params=_params(("arbitrary",)),
        name="rglru",
    )(proj, proj, conv_w, conv_b, w_a, b_a, w_i, b_i, lam)


def _out_proj_kernel(x_ref, att_ref, rnn_ref, wo_ref, g_ref, wr_ref,
                     h_ref, hn_ref, route_ref, counts_ref, carry_ref, logits_ref, *, n_tiles, d_att, n_experts):
    i = pl.program_id(0)

    @pl.when(i == 0)
    def _():
        carry_ref[...] = jnp.zeros_like(carry_ref)
        logits_ref[...] = jnp.zeros_like(logits_ref)

    logits = logits_ref[...]
    tm = logits.shape[0]
    lane = lax.broadcasted_iota(jnp.int32, logits.shape, 1)
    neg = -jnp.inf

    def first_argmax(vals, vmax):
        return jnp.min(jnp.where(vals == vmax, lane, LANES), axis=-1, keepdims=True)

    g_mask = (lane >= n_experts) & (lane < n_experts + N_GROUPS)
    g_log = jnp.where(g_mask, logits, neg)
    g_max = jnp.max(g_log, axis=-1, keepdims=True)
    g_w = 1.0 / jnp.sum(jnp.exp(g_log - g_max), axis=-1, keepdims=True)
    g_idx = first_argmax(g_log, g_max) - n_experts

    e_lo = g_idx * EXPERTS_PER_GROUP
    e_log = jnp.where((lane >= e_lo) & (lane < e_lo + EXPERTS_PER_GROUP), logits, neg)
    l1 = jnp.max(e_log, axis=-1, keepdims=True)
    i1 = first_argmax(e_log, l1)
    e_log2 = jnp.where(lane == i1, neg, e_log)
    l2 = jnp.max(e_log2, axis=-1, keepdims=True)
    i2 = first_argmax(e_log2, l2)
    e2 = jnp.exp(l2 - l1)
    w1 = g_w / (1.0 + e2)
    w2 = g_w * e2 / (1.0 + e2)

    hot1 = lane == i1
    hot2 = lane == i2
    hot = jnp.where(hot1 | hot2, 1.0, 0.0)
    before = (lax.broadcasted_iota(jnp.int32, (tm, tm), 0)
              > lax.broadcasted_iota(jnp.int32, (tm, tm), 1)).astype(BF16)
    seen = jnp.dot(before, hot.astype(BF16), preferred_element_type=F32) + carry_ref[...]
    rank1 = jnp.sum(jnp.where(hot1, seen, 0.0), axis=-1, keepdims=True)
    rank2 = jnp.sum(jnp.where(hot2, seen, 0.0), axis=-1, keepdims=True)
    carry_ref[...] += jnp.where(i > 0, jnp.sum(hot, axis=0, keepdims=True), 0.0)
    counts_ref[...] = jnp.broadcast_to(carry_ref[...], counts_ref.shape)

    rec = jnp.zeros(logits.shape, F32)
    for lane_idx, col in ((R_E1, i1.astype(F32)), (R_E2, i2.astype(F32)), (R_W1, w1), (R_W2, w2),
                          (R_RANK1, rank1), (R_RANK2, rank2)):
        rec = jnp.where(lane == lane_idx, col, rec)
    route_ref[...] = rec

    h = (x_ref[...]
         + jnp.dot(att_ref[...], wo_ref[0:d_att, :], preferred_element_type=F32)
         + jnp.dot(rnn_ref[...], wo_ref[d_att:, :], preferred_element_type=F32))
    h_ref[...] = h
    hn = _rms(h, g_ref[...])
    hn_ref[...] = hn
    logits_ref[...] = jnp.dot(hn.astype(BF16), wr_ref[...], preferred_element_type=F32)


def _out_proj(x2, att, rnn, w_out, gain, w_router, *, tm, n_experts):
    t, d = x2.shape
    d_att = att.shape[1]
    d_rnn = rnn.shape[1]
    n_tiles = t // tm
    row = lambda i: (jnp.minimum(i, n_tiles - 1), 0)
    prev_row = lambda i: (jnp.maximum(i - 1, 0), 0)
    fixed = lambda i: (0, 0)
    return pl.pallas_call(
        functools.partial(_out_proj_kernel, n_tiles=n_tiles, d_att=d_att, n_experts=n_experts),
        grid=(n_tiles + 1,),
        in_specs=[
            pl.BlockSpec((tm, d), row),
            pl.BlockSpec((tm, d_att), row),
            pl.BlockSpec((tm, d_rnn), row),
            pl.BlockSpec((d_att + d_rnn, d), fixed),
            pl.BlockSpec((1, d), fixed),
            pl.BlockSpec((d, LANES), fixed),
        ],
        out_specs=[pl.BlockSpec((tm, d), row), pl.BlockSpec((tm, d), row),
                   pl.BlockSpec((tm, LANES), prev_row), pl.BlockSpec((SUBLANES, LANES), fixed)],
        out_shape=[jax.ShapeDtypeStruct((t, d), F32), jax.ShapeDtypeStruct((t, d), F32),
                   jax.ShapeDtypeStruct((t, LANES), F32), jax.ShapeDtypeStruct((SUBLANES, LANES), F32)],
        scratch_shapes=[pltpu.VMEM((1, LANES), F32), pltpu.VMEM((tm, LANES), F32)],
        compiler_params=_params(("arbitrary",)),
        name="out_proj",
    )(x2, att, rnn, w_out, gain, w_router)


def _row_copy(src_hbm, src_row, dst_ref, dst_row, sem):
    return pltpu.make_async_copy(src_hbm.at[pl.ds(src_row, 1)], dst_ref.at[pl.ds(dst_row, 1)], sem)


def _dispatch_kernel(p1_ref, p2_ref, ztile_ref, hn_ref, xs_hbm, zero_ref, zsem, sem):
    i = pl.program_id(0)
    tm = hn_ref.shape[0]
    tile_rows = zero_ref.shape[0]

    @pl.when(i == 0)
    def _():
        zero_ref[...] = jnp.zeros_like(zero_ref)

        def zero_copy(z):
            start = pl.multiple_of(ztile_ref[z] * tile_rows, tile_rows)
            return pltpu.make_async_copy(zero_ref, xs_hbm.at[pl.ds(start, tile_rows)], zsem)

        for z in range(ztile_ref.shape[0]):
            @pl.when(ztile_ref[z] >= 0)
            def _():
                zero_copy(z).start()
        for z in range(ztile_ref.shape[0]):
            @pl.when(ztile_ref[z] >= 0)
            def _():
                zero_copy(z).wait()

    def issue(t, carry):
        _row_copy(hn_ref, t, xs_hbm, p1_ref[t], sem).start(priority=0)
        _row_copy(hn_ref, t, xs_hbm, p2_ref[t], sem).start(priority=1)
        return carry

    lax.fori_loop(0, tm, issue, 0, unroll=DMA_UNROLL)

    def drain(t, carry):
        _row_copy(hn_ref, 0, xs_hbm, 0, sem).wait()
        return carry

    lax.fori_loop(0, TOP_K * tm, drain, 0, unroll=DMA_UNROLL)


def _moe_dispatch(pos1, pos2, ztile, hn, *, n_rows, tile_rows, tm):
    t, width = hn.shape
    smem_blk = pl.BlockSpec((tm,), lambda i: (i,), memory_space=pltpu.SMEM)
    return pl.pallas_call(
        _dispatch_kernel,
        grid=(t // tm,),
        in_specs=[smem_blk, smem_blk, pl.BlockSpec(memory_space=pltpu.SMEM),
                  pl.BlockSpec((tm, width), lambda i: (i, 0))],
        out_specs=pl.BlockSpec(memory_space=pl.ANY),
        out_shape=jax.ShapeDtypeStruct((n_rows, width), hn.dtype),
        scratch_shapes=[pltpu.VMEM((tile_rows, width), hn.dtype), pltpu.SemaphoreType.DMA,
                        pltpu.SemaphoreType.DMA],
        compiler_params=_params(("arbitrary",)),
        name="moe_dispatch",
    )(pos1, pos2, ztile, hn)


def _experts_kernel(texp_ref, nused_ref, x_ref, wg_ref, wu_ref, wd_ref, y_ref, wgu_bf_ref, wd_bf_ref):
    i = pl.program_id(0)
    d_e = wd_ref.shape[1]

    @pl.when((i == 0) | (texp_ref[i] != texp_ref[jnp.maximum(i - 1, 0)]))
    def _():
        wgu_bf_ref[:, :d_e] = wg_ref[0].astype(BF16)
        wgu_bf_ref[:, d_e:] = wu_ref[0].astype(BF16)
        wd_bf_ref[...] = wd_ref[0].astype(BF16)

    @pl.when(i < nused_ref[0])
    def _():
        gu = jnp.dot(x_ref[...].astype(BF16), wgu_bf_ref[...], preferred_element_type=F32)
        hid = jax.nn.silu(gu[:, :d_e]) * gu[:, d_e:]
        y_ref[...] = jnp.dot(hid.astype(BF16), wd_bf_ref[...], preferred_element_type=F32)

    @pl.when(i >= nused_ref[0])
    def _():
        y_ref[...] = jnp.zeros_like(y_ref)


def _moe_experts(tile_expert, n_used, xs, w_gate, w_up, w_down, *, tile_rows):
    n_rows, width = xs.shape
    n_tiles = n_rows // tile_rows
    _, d, d_e = w_gate.shape
    row = lambda i, te, nu: (jnp.minimum(i, nu[0] - 1), 0)
    wmap = lambda i, te, nu: (te[i], 0, 0)
    grid_spec = pltpu.PrefetchScalarGridSpec(
        num_scalar_prefetch=2,
        grid=(n_tiles,),
        in_specs=[pl.BlockSpec((tile_rows, width), row),
                  pl.BlockSpec((1, d, d_e), wmap),
                  pl.BlockSpec((1, d, d_e), wmap),
                  pl.BlockSpec((1, d_e, d), wmap)],
        out_specs=pl.BlockSpec((tile_rows, width), lambda i, te, nu: (i, 0)),
        scratch_shapes=[pltpu.VMEM((d, 2 * d_e), BF16), pltpu.VMEM((d_e, d), BF16)],
    )
    return pl.pallas_call(
        _experts_kernel,
        grid_spec=grid_spec,
        out_shape=jax.ShapeDtypeStruct((n_rows, width), xs.dtype),
        compiler_params=_params(("arbitrary",)),
        name="moe_experts",
    )(tile_expert, n_used, xs, w_gate, w_up, w_down)


def _moe_ple_kernel(p1_ref, p2_ref, p1n_ref, p2n_ref,
                    y_hbm, route_ref, h_ref, p_ref, g_ref, wg_ref, wu_ref, o_ref,
                    buf1_ref, buf2_ref, sem):
    i = pl.program_id(0)
    n = pl.num_programs(0)
    tm, d = h_ref.shape
    slot = i % 2
    nxt = 1 - slot

    def start_rows(p1s, p2s, t, slot_idx):
        _row_copy(y_hbm, p1s[t], buf1_ref.at[slot_idx], t, sem.at[slot_idx]).start(priority=0)
        _row_copy(y_hbm, p2s[t], buf2_ref.at[slot_idx], t, sem.at[slot_idx]).start(priority=1)

    def drain(slot_idx):
        def body(t, carry):
            _row_copy(y_hbm, 0, buf1_ref.at[slot_idx], 0, sem.at[slot_idx]).wait()
            return carry
        lax.fori_loop(0, TOP_K * tm, body, 0, unroll=DMA_UNROLL)

    @pl.when(i == 0)
    def _():
        def first(t, carry):
            start_rows(p1_ref, p2_ref, t, 0)
            return carry
        lax.fori_loop(0, tm, first, 0, unroll=DMA_UNROLL)

    drain(slot)

    rec = route_ref[...]
    w1 = _lane_pick(rec, R_W1)
    w2 = _lane_pick(rec, R_W2)
    h = h_ref[...] + (w1 * buf1_ref[slot] + w2 * buf2_ref[slot])
    hn = _rms(h, g_ref[...]).astype(BF16)
    up = jnp.dot(p_ref[...].astype(BF16), wu_ref[...], preferred_element_type=F32)

    cw = d // PLE_CHUNKS
    rows = tm // PLE_CHUNKS
    for c in range(PLE_CHUNKS):
        sl = slice(c * cw, (c + 1) * cw)
        gate = jax.nn.sigmoid(jnp.dot(hn, wg_ref[:, sl], preferred_element_type=F32))
        o_ref[:, sl] = h[:, sl] + gate * up[:, sl]
        for t in range(c * rows, (c + 1) * rows):
            start_rows(p1n_ref, p2n_ref, t, nxt)

    @pl.when(i == n - 1)
    def _():
        drain(nxt)


def _moe_ple(pos1, pos2, y, route, h, p2, gain, w_gate, w_up, *, tm):
    t, d = h.shape
    d_ple = p2.shape[1]
    width = y.shape[1]
    n = t // tm
    row = lambda i: (i, 0)
    fixed = lambda i: (0, 0)
    cur = pl.BlockSpec((tm,), lambda i: (i,), memory_space=pltpu.SMEM)
    nxt = pl.BlockSpec((tm,), lambda i: (jnp.minimum(i + 1, n - 1),), memory_space=pltpu.SMEM)
    return pl.pallas_call(
        _moe_ple_kernel,
        grid=(n,),
        in_specs=[cur, cur, nxt, nxt,
                  pl.BlockSpec(memory_space=pl.ANY),
                  pl.BlockSpec((tm, LANES), row),
                  pl.BlockSpec((tm, d), row),
                  pl.BlockSpec((tm, d_ple), row),
                  pl.BlockSpec((1, d), fixed),
                  pl.BlockSpec((d, d), fixed),
                  pl.BlockSpec((d_ple, d), fixed)],
        out_specs=pl.BlockSpec((tm, d), row),
        out_shape=jax.ShapeDtypeStruct((t, d), F32),
        scratch_shapes=[pltpu.VMEM((2, tm, width), y.dtype), pltpu.VMEM((2, tm, width), y.dtype),
                        pltpu.SemaphoreType.DMA((2,))],
        compiler_params=_params(("arbitrary",)),
        name="moe_ple",
    )(pos1, pos2, pos1, pos2, y, route, h, p2, gain, w_gate, w_up)


def _tile(n, want):
    if n <= want:
        return n
    for cand in range(want, 0, -SUBLANES):
        if n % cand == 0:
            return cand
    return n


def kernel(x, p, mix_norm, w_in, b_forget, q_norm, k_norm, conv_w, conv_b, w_rec_gate, b_rec_gate,
           w_in_gate, b_in_gate, lru_lambda, w_out, ffn_norm, w_router_group, w_router_expert,
           w_expert_gate, w_expert_up, w_expert_down, ple_norm, w_ple_gate, w_ple_up):
    b, s, d = x.shape
    depth = w_in.shape[0]
    n_heads = b_forget.shape[1]
    d_att = n_heads * HEAD_DIM
    d_rnn = conv_w.shape[2]
    n_experts = w_router_expert.shape[2]
    t = b * s
    col_f = 3 * d_att
    col_rx = col_f + n_heads
    assert d_att == d_rnn, "the fused projection output is indexed in equal-width column groups"

    tm_in = _tile(t, 1024)
    tm = _tile(t, 512)
    tq = _tile(s, 512)
    tc_lru = _tile(s, 256)
    tm_disp = _tile(t, 1024)
    tm_comb = _tile(t, 256)
    tile_rows = _tile(t, 512)
    n_rows = TOP_K * t + n_experts * tile_rows
    n_tiles = n_rows // tile_rows

    h = x.reshape(t, d)
    for li in range(depth):
        w = w_in[li]
        w_main = jnp.concatenate([w[:, :col_f], w[:, col_rx:]], axis=1).astype(BF16)
        w_f = jnp.pad(w[:, col_f:col_rx], ((0, 0), (0, LANES - n_heads))).astype(BF16)
        scale = HEAD_DIM ** -0.5
        q_gain = (q_norm[li] * scale).reshape(1, HEAD_DIM)
        k_gain = k_norm[li].reshape(1, HEAD_DIM)
        proj, f_logits = _in_proj(h, mix_norm[li].reshape(1, d), w_main, w_f, q_gain, k_gain, tm=tm_in)
        proj = proj.reshape(b, s, -1)

        b_pad = jnp.pad(b_forget[li], (0, LANES - n_heads)).reshape(1, LANES)
        ct = _forget_cum(f_logits.reshape(b, s, LANES), b_pad, tc=tq)
        c_first = ct[:, :, :, 0].transpose(0, 2, 1).reshape(-1)
        c_last = ct[:, :, :, tq - 1].transpose(0, 2, 1).reshape(-1)
        qk_bound = (2.0 * 1.02 * HEAD_DIM * scale * jnp.max(jnp.abs(q_norm[li]))
                    * jnp.max(jnp.abs(k_norm[li]))).reshape(1).astype(F32)
        att = _fox_attn(proj, ct, c_first, c_last, qk_bound, n_heads=n_heads, tq=tq)

        rnn = _rglru(proj, conv_w[li], conv_b[li].reshape(1, d_rnn), w_rec_gate[li].astype(BF16),
                     b_rec_gate[li].reshape(1, d_rnn), w_in_gate[li].astype(BF16),
                     b_in_gate[li].reshape(1, d_rnn), lru_lambda[li].reshape(1, d_rnn), tc=tc_lru)

        w_router = jnp.pad(jnp.concatenate([w_router_expert[li], w_router_group[li]], axis=1),
                           ((0, 0), (0, LANES - n_experts - N_GROUPS)))
        h1, hn, route, counts = _out_proj(h, att.reshape(t, d_att), rnn.reshape(t, d_rnn),
                                          w_out[li].astype(BF16), ffn_norm[li].reshape(1, d),
                                          w_router.astype(BF16), tm=tm, n_experts=n_experts)

        cnt = counts[0, :n_experts].astype(jnp.int32)
        padded = (cnt + tile_rows - 1) // tile_rows * tile_rows
        ends = jnp.cumsum(padded)
        offsets = ends - padded
        n_used = (ends[-1] // tile_rows).reshape(1)
        tile_start = jnp.minimum(jnp.arange(n_tiles, dtype=jnp.int32), n_used[0] - 1) * tile_rows
        tile_expert = jnp.sum(ends[None, :] <= tile_start[:, None], axis=1).astype(jnp.int32)
        seg_last = jnp.where(padded > 0, ends // tile_rows - 1, -1)
        tail = n_used[0] + jnp.arange(n_tiles - TOP_K * t // tile_rows, dtype=jnp.int32)
        ztile = jnp.concatenate([seg_last, jnp.where(tail < n_tiles, tail, -1)]).astype(jnp.int32)

        def slot_row(expert_lane, rank_lane):
            e = route[:, expert_lane].astype(jnp.int32)
            hot = e[:, None] == jnp.arange(n_experts, dtype=jnp.int32)[None, :]
            return route[:, rank_lane].astype(jnp.int32) + jnp.sum(jnp.where(hot, offsets[None, :], 0), axis=1)

        pos1 = slot_row(R_E1, R_RANK1)
        pos2 = slot_row(R_E2, R_RANK2)

        xs = _moe_dispatch(pos1, pos2, ztile, hn, n_rows=n_rows, tile_rows=tile_rows, tm=tm_disp)
        y = _moe_experts(tile_expert, n_used, xs, w_expert_gate[li], w_expert_up[li], w_expert_down[li],
                         tile_rows=tile_rows)
        h = _moe_ple(pos1, pos2, y, route, h1, p[li].reshape(t, -1),
                     ple_norm[li].reshape(1, d), w_ple_gate[li].astype(BF16),
                     w_ple_up[li].astype(BF16), tm=tm_comb)
    return h.reshape(b, s, d)
```
